```python
import math
import functools
import jax
import jax.numpy as jnp
from jax import lax
import numpy as np

D_MODEL = 2048
BATCH = 8
SEQ = 2048
DEPTH = 1
DEC_BATCH = 128
DEC_SEQ = 4
PAST_LEN = 16384
PAGE_SIZE = 128

MLA_HEADS = 8
QK_NOPE = 128
QK_ROPE = 64
QK_HEAD = QK_NOPE + QK_ROPE
V_HEAD = 128
Q_LORA = 512
KV_LORA = 512
CACHE_W = KV_LORA + QK_ROPE
ROPE_THETA = 10000.0
ATTN_SCALE = QK_HEAD ** -0.5
Q_BLOCK = 128
HG_HEADS = 8
HG_DK = 128
HG_DV = 128
HG_CHUNK = 64
MIX_WIDTH = MLA_HEADS * V_HEAD + HG_HEADS * HG_DV
IN_SIZES = (Q_LORA, KV_LORA, QK_ROPE, HG_HEADS * HG_DK, HG_HEADS * HG_DK, HG_HEADS * HG_DV, HG_HEADS * HG_DV)
D_IN = Q_LORA + KV_LORA + QK_ROPE + 2 * HG_HEADS * HG_DK + 2 * HG_HEADS * HG_DV
N_EXPERTS = 32
TOP_K = 4
D_EXPERT = D_MODEL
SWIGLU_LIMIT = 7.0
SWIGLU_ALPHA = 1.702
MOE_BLOCK = 128
NORM_EPS = 1e-6
NEG_INF = -1e30

kernel_name = 'hymba_mla_hgrn2_moe_adaln_step'


def rms_norm(x, w):
    xf = x.astype(jnp.float32)
    y = xf * lax.rsqrt(jnp.mean(xf * xf, axis=-1, keepdims=True) + NORM_EPS)
    return (y * w.astype(jnp.float32)).astype(x.dtype)


def apply_rope(x, pos):
    half = QK_ROPE // 2
    inv = ROPE_THETA ** (-jnp.arange(half, dtype=jnp.float32) / half)
    ang = pos.astype(jnp.float32)[:, None] * inv[None, :]
    cos = jnp.cos(ang)[:, None, :]
    sin = jnp.sin(ang)[:, None, :]
    xf = x.astype(jnp.float32)
    x1, x2 = xf[..., :half], xf[..., half:]
    return jnp.concatenate([x1 * cos - x2 * sin, x2 * cos + x1 * sin], -1).astype(x.dtype)


def qk_head(nope, rope_part, pos, w):
    hd = rms_norm(jnp.concatenate([nope, rope_part], -1), w)
    return jnp.concatenate([hd[..., :QK_NOPE], apply_rope(hd[..., QK_NOPE:], pos)], -1)


def mla_queries(cq, pos, q_a_norm_w, w_uq, q_head_norm_w):
    cq = rms_norm(cq, q_a_norm_w)
    q = jnp.einsum('nsr,rhd->nshd', cq, w_uq)
    return qk_head(q[..., :QK_NOPE], q[..., QK_NOPE:], pos, q_head_norm_w)


def mla_keys(lat, kr, pos, w_uk, k_w):
    k_nope = jnp.einsum('nsc,chd->nshd', lat, w_uk)
    kr_h = jnp.broadcast_to(kr[:, :, None, :], k_nope.shape[:-1] + (QK_ROPE,))
    return qk_head(k_nope, kr_h, pos, k_w)


def attend_prompt(q, lat, kr, w_ukv, k_w, pos):
    w_uk, w_uv = w_ukv[..., :QK_NOPE], w_ukv[..., QK_NOPE:]
    k = mla_keys(lat, kr, pos, w_uk, k_w)
    v = jnp.einsum('nsc,chd->nshd', lat, w_uv)
    N, S = q.shape[:2]
    qb_len = math.gcd(S, Q_BLOCK)
    nb = S // qb_len
    qb = q.reshape(N, nb, qb_len, MLA_HEADS, QK_HEAD).transpose(1, 0, 2, 3, 4)
    kpos = jnp.arange(S)

    def one_block(args):
        qi, b = args
        s = jnp.einsum('nqhd,nkhd->nhqk', qi, k).astype(jnp.float32) * ATTN_SCALE
        qpos = b * qb_len + jnp.arange(qb_len)
        s = jnp.where(kpos[None, :] <= qpos[:, None], s, NEG_INF)
        p = jax.nn.softmax(s, axis=-1).astype(v.dtype)
        return jnp.einsum('nhqk,nkhd->nqhd', p, v)

    o = lax.map(one_block, (qb, jnp.arange(nb)))
    return o.transpose(1, 0, 2, 3, 4).reshape(N, S, MLA_HEADS, V_HEAD)


def attend_sample(q, lat, kr, w_ukv, k_w, cache, layer_idx, page_table):
    w_uk, w_uv = w_ukv[..., :QK_NOPE], w_ukv[..., QK_NOPE:]
    T = q.shape[1]
    k_new = mla_keys(lat, kr, PAST_LEN + jnp.arange(T), w_uk, k_w)
    s = jnp.einsum('bthd,bshd->bhts', q, k_new).astype(jnp.float32) * ATTN_SCALE
    s = jnp.where(jnp.arange(T)[None, :] <= jnp.arange(T)[:, None], s, NEG_INF)
    m = jnp.max(s, axis=-1)
    p = jnp.exp(s - m[..., None])
    l = jnp.sum(p, axis=-1)
    acc = jnp.einsum('bhts,bsc->bhtc', p, lat.astype(jnp.float32))

    def page_step(carry, inp):
        m, l, acc = carry
        phys, pidx = inp
        blk = cache[layer_idx, phys]
        lat_p, kr_p = blk[..., :KV_LORA], blk[..., KV_LORA:]
        k = mla_keys(lat_p, kr_p, pidx * PAGE_SIZE + jnp.arange(PAGE_SIZE), w_uk, k_w)
        sp = jnp.einsum('bthd,bshd->bhts', q, k).astype(jnp.float32) * ATTN_SCALE
        m_new = jnp.maximum(m, jnp.max(sp, axis=-1))
        alpha = jnp.exp(m - m_new)
        pp = jnp.exp(sp - m_new[..., None])
        l = l * alpha + jnp.sum(pp, axis=-1)
        acc = acc * alpha[..., None] + jnp.einsum('bhts,bsc->bhtc', pp, lat_p.astype(jnp.float32))
        return (m_new, l, acc), None

    n_pages = page_table.shape[1]
    (m, l, acc), _ = lax.scan(page_step, (m, l, acc), (page_table.T, jnp.arange(n_pages)))
    lat_mix = (acc / l[..., None]).astype(q.dtype)
    return jnp.einsum('bhtc,chd->bthd', lat_mix, w_uv)


def gla_chunked(q, k, v, log_f, S0):
    N, T, H, DK = q.shape
    DV = v.shape[-1]
    C = math.gcd(T, HG_CHUNK)
    nc = T // C

    def to_chunks(a):
        return a.reshape(N, nc, C, H, a.shape[-1]).transpose(1, 0, 3, 2, 4)

    mask = jnp.tril(jnp.ones((C, C), dtype=bool))[:, :, None]

    def step(S, inp):
        qc, kc, vc, gc = inp
        G = jnp.cumsum(gc, axis=2)
        o_inter = jnp.einsum('nhtk,nhkv->nhtv', qc * jnp.exp(G), S)
        diff = G[:, :, :, None, :] - G[:, :, None, :, :]
        decay = jnp.where(mask, jnp.exp(jnp.where(mask, diff, 0.0)), 0.0)
        A = jnp.einsum('nhtk,nhtsk,nhsk->nhts', qc, decay, kc)
        o = o_inter + jnp.einsum('nhts,nhsv->nhtv', A, vc)
        G_last = G[:, :, -1, :]
        S_new = jnp.exp(G_last)[..., None] * S + jnp.einsum(
            'nhsk,nhsv->nhkv', kc * jnp.exp(G_last[:, :, None, :] - G), vc)
        return S_new, o

    S, o = lax.scan(step, S0, (to_chunks(q), to_chunks(k), to_chunks(v), to_chunks(log_f)))
    return o.transpose(1, 0, 3, 2, 4).reshape(N, T, H, DV), S


def hgrn2_mix(hq, hf, hi, hg, lb, S0, g_norm_w):
    N, T, _ = hq.shape
    f32 = jnp.float32
    q = jax.nn.silu(hq.astype(f32)).reshape(N, T, HG_HEADS, HG_DK)
    z = hf.astype(f32).reshape(N, T, HG_HEADS, HG_DK)
    lbh = lb.reshape(HG_HEADS, HG_DK)
    log_f = jnp.logaddexp(jnp.log(lbh), jnp.log1p(-lbh) + jax.nn.log_sigmoid(z))
    k = -jnp.expm1(log_f)
    v = hi.astype(f32).reshape(N, T, HG_HEADS, HG_DV)
    o, S = gla_chunked(q, k, v, log_f, S0.astype(f32))
    gate = jax.nn.silu(hg.astype(f32)).reshape(N, T, HG_HEADS, HG_DV)
    o = rms_norm(o, g_norm_w) * gate
    return o.reshape(N, T, HG_HEADS * HG_DV).astype(hq.dtype), S.astype(S0.dtype)


def moe_ffn(h, router_w, router_b, w1, b1, w2, b2):
    shape = h.shape
    xt = h.reshape(-1, shape[-1])
    N = xt.shape[0]
    logits = (xt @ router_w).astype(jnp.float32) + router_b.astype(jnp.float32)
    top_v, top_i = lax.top_k(logits, TOP_K)
    gates = jax.nn.softmax(top_v, axis=-1)
    A = N * TOP_K
    flat_e = top_i.reshape(-1)
    flat_tok = jnp.repeat(jnp.arange(N, dtype=jnp.int32), TOP_K)
    order = jnp.argsort(flat_e)
    e_sorted = flat_e[order]
    counts = jnp.bincount(flat_e, length=N_EXPERTS)
    padded = (counts + MOE_BLOCK - 1) // MOE_BLOCK * MOE_BLOCK
    start = jnp.cumsum(counts) - counts
    pend = jnp.cumsum(padded)
    pstart = pend - padded
    dest = pstart[e_sorted] + jnp.arange(A) - start[e_sorted]
    n_blocks = -(-A // MOE_BLOCK) + N_EXPERTS
    P = n_blocks * MOE_BLOCK
    slot_tok = jnp.full((P,), N, jnp.int32).at[dest].set(flat_tok[order])
    slot_gate = jnp.zeros((P,), jnp.float32).at[dest].set(gates.reshape(-1)[order])
    block_e = jnp.minimum(jnp.searchsorted(pend, jnp.arange(n_blocks) * MOE_BLOCK, side='right'), N_EXPERTS - 1)
    x_pad = jnp.concatenate([xt, jnp.zeros((1, xt.shape[1]), xt.dtype)], 0)

    def block_step(acc, inp):
        tok_b, gate_b, e = inp
        y = x_pad[tok_b] @ w1[e] + b1[e]
        glu = jnp.minimum(y[:, :D_EXPERT], SWIGLU_LIMIT)
        lin = jnp.clip(y[:, D_EXPERT:], -SWIGLU_LIMIT, SWIGLU_LIMIT)
        a = glu * jax.nn.sigmoid(SWIGLU_ALPHA * glu) * (lin + 1)
        out = a @ w2[e] + b2[e]
        return acc.at[tok_b].add(out.astype(jnp.float32) * gate_b[:, None]), None

    acc, _ = lax.scan(block_step, jnp.zeros((N + 1, xt.shape[1]), jnp.float32),
                      (slot_tok.reshape(n_blocks, MOE_BLOCK), slot_gate.reshape(n_blocks, MOE_BLOCK), block_e))
    return acc[:N].astype(h.dtype).reshape(shape)


def adaln(c, w_ada, b_ada):
    mod = jax.nn.silu(c) @ w_ada + b_ada
    return jnp.split(mod[:, None, :], 6, axis=-1)


def split_in(z):
    idx = np.cumsum(np.array(IN_SIZES))[:-1].tolist()
    return jnp.split(z, idx, axis=-1)


def layer(x, c, pos, attend, S0, lb, p):
    N, T, _ = x.shape
    sh_a, sc_a, g_a, sh_f, sc_f, g_f = adaln(c, p['w_ada'], p['b_ada'])
    h = rms_norm(x, p['attn_norm_w']) * (1 + sc_a) + sh_a
    cq, ckv, kr, hq, hf, hi, hg = split_in(h @ p['w_in'])
    lat = rms_norm(ckv, p['kv_a_norm_w'])
    q = mla_queries(cq, pos, p['q_a_norm_w'], p['w_uq'], p['q_head_norm_w'])
    o_attn = attend(q, lat, kr, p['w_ukv'], p['k_head_norm_w']).reshape(N, T, MLA_HEADS * V_HEAD)
    o_attn = rms_norm(o_attn, p['attn_group_norm_w'])
    o_rec, S = hgrn2_mix(hq, hf, hi, hg, lb, S0, p['hg_norm_w'])
    x = x + g_a * (jnp.concatenate([o_attn, o_rec], -1) @ p['w_out'])
    h2 = rms_norm(x, p['ffn_norm_w']) * (1 + sc_f) + sh_f
    x = x + g_f * moe_ffn(h2, p['router_w'], p['router_b'], p['w1'], p['b1'], p['w2'], p['b2'])
    return x, jnp.concatenate([lat, kr], -1), S


def setup_inputs(seed: int = 0) -> dict:
    key = jax.random.key(seed)
    ks = jax.random.split(key, 32)
    f32 = jnp.float32

    def nrm(k, shape, scale):
        return jax.random.normal(k, shape, f32) * scale

    n_pages = PAST_LEN // PAGE_SIZE
    n_used = DEC_BATCH * n_pages
    n_pool = n_used + (n_used + 3) // 4
    page_table = jax.random.permutation(ks[4], n_pool)[:n_used].reshape(DEC_BATCH, n_pages).astype(jnp.int32)
    D = D_MODEL
    return {
        'x_prompt': nrm(ks[0], (BATCH, SEQ, D), 1.0),
        'x_sample': nrm(ks[1], (DEC_BATCH, DEC_SEQ, D), 1.0),
        'cache_mla': nrm(ks[2], (DEPTH, n_pool, PAGE_SIZE, CACHE_W), 1.0),
        'state_hgrn': nrm(ks[3], (DEPTH, DEC_BATCH, HG_HEADS, HG_DK, HG_DV), 0.5),
        'page_table': page_table,
        'c_prompt': nrm(ks[5], (BATCH, D), 1.0),
        'c_sample': nrm(ks[6], (DEC_BATCH, D), 1.0),
        'w_ada': nrm(ks[7], (DEPTH, D, 6 * D), 0.5 * D ** -0.5),
        'b_ada': nrm(ks[8], (DEPTH, 6 * D), 0.02),
        'attn_norm_w': 1.0 + nrm(ks[9], (DEPTH, D), 0.02),
        'w_in': nrm(ks[10], (DEPTH, D, D_IN), D ** -0.5),
        'q_a_norm_w': 1.0 + nrm(ks[11], (DEPTH, Q_LORA), 0.02),
        'w_uq': nrm(ks[12], (DEPTH, Q_LORA, MLA_HEADS, QK_HEAD), Q_LORA ** -0.5),
        'kv_a_norm_w': 1.0 + nrm(ks[13], (DEPTH, KV_LORA), 0.02),
        'w_ukv': nrm(ks[14], (DEPTH, KV_LORA, MLA_HEADS, QK_NOPE + V_HEAD), KV_LORA ** -0.5),
        'q_head_norm_w': 1.0 + nrm(ks[15], (DEPTH, QK_HEAD), 0.02),
        'k_head_norm_w': 1.0 + nrm(ks[16], (DEPTH, QK_HEAD), 0.02),
        'attn_group_norm_w': 1.0 + nrm(ks[17], (DEPTH, MLA_HEADS * V_HEAD), 0.02),
        'hg_lower_bound': nrm(ks[18], (DEPTH + 1, HG_HEADS * HG_DK), 0.5),
        'hg_norm_w': 1.0 + nrm(ks[19], (DEPTH, HG_DV), 0.02),
        'w_out': nrm(ks[20], (DEPTH, MIX_WIDTH, D), MIX_WIDTH ** -0.5),
        'ffn_norm_w': 1.0 + nrm(ks[21], (DEPTH, D), 0.02),
        'router_w': nrm(ks[22], (DEPTH, D, N_EXPERTS), D ** -0.5),
        'router_b': nrm(ks[23], (DEPTH, N_EXPERTS), 0.01),
        'w1': nrm(ks[24], (DEPTH, N_EXPERTS, D, 2 * D_EXPERT), D ** -0.5),
        'b1': nrm(ks[25], (DEPTH, N_EXPERTS, 2 * D_EXPERT), 0.01),
        'w2': nrm(ks[26], (DEPTH, N_EXPERTS, D_EXPERT, D), D_EXPERT ** -0.5),
        'b2': nrm(ks[27], (DEPTH, N_EXPERTS, D), 0.01),
    }


def reference(x_prompt, x_sample, cache_mla, state_hgrn, page_table, c_prompt, c_sample,
              w_ada, b_ada, attn_norm_w, w_in, q_a_norm_w, w_uq, kv_a_norm_w, w_ukv,
              q_head_norm_w, k_head_norm_w, attn_group_norm_w, hg_lower_bound, hg_norm_w,
              w_out, ffn_norm_w, router_w, router_b, w1, b1, w2, b2):
    lb_all = jnp.cumsum(jax.nn.softmax(hg_lower_bound.astype(jnp.float32), axis=0), axis=0)
    pos_p = jnp.arange(x_prompt.shape[1])
    pos_s = PAST_LEN + jnp.arange(x_sample.shape[1])
    xp, xs = x_prompt, x_sample
    rows_p, rows_s, st_p, st_s = [], [], [], []
    for l in range(DEPTH):
        p = dict(w_ada=w_ada[l], b_ada=b_ada[l], attn_norm_w=attn_norm_w[l], w_in=w_in[l],
                 q_a_norm_w=q_a_norm_w[l], w_uq=w_uq[l], kv_a_norm_w=kv_a_norm_w[l], w_ukv=w_ukv[l],
                 q_head_norm_w=q_head_norm_w[l], k_head_norm_w=k_head_norm_w[l],
                 attn_group_norm_w=attn_group_norm_w[l], hg_norm_w=hg_norm_w[l], w_out=w_out[l],
                 ffn_norm_w=ffn_norm_w[l], router_w=router_w[l], router_b=router_b[l],
                 w1=w1[l], b1=b1[l], w2=w2[l], b2=b2[l])
        S0_p = jnp.zeros((xp.shape[0], HG_HEADS, HG_DK, HG_DV), xp.dtype)
        xp, r_p, s_p = layer(xp, c_prompt, pos_p, functools.partial(attend_prompt, pos=pos_p),
                             S0_p, lb_all[l], p)
        att_s = functools.partial(attend_sample, cache=cache_mla, layer_idx=l, page_table=page_table)
        xs, r_s, s_s = layer(xs, c_sample, pos_s, att_s, state_hgrn[l], lb_all[l], p)
        rows_p.append(r_p)
        rows_s.append(r_s)
        st_p.append(s_p)
        st_s.append(s_s)
    return (xp, xs, jnp.stack(rows_p), jnp.stack(rows_s), jnp.stack(st_p), jnp.stack(st_s))
```

```python
import functools

import jax
import jax.numpy as jnp
from jax import lax
from jax.experimental import pallas as pl
from jax.experimental.pallas import tpu as pltpu

F32 = jnp.float32
BF16 = jnp.bfloat16

NORM_EPS = 1e-6
NEG_INF = -1e30
ROPE_THETA = 10000.0
TOP_K = 4
SWIGLU_LIMIT = 7.0
SWIGLU_ALPHA = 1.702
HG_CHUNK = 64
HG_SUB = 16
LANES = 128
SUBLANES = 8
VMEM_LIMIT_BYTES = 56 * 1024 * 1024


def _cparams(*sem):
    return pltpu.CompilerParams(dimension_semantics=sem, vmem_limit_bytes=VMEM_LIMIT_BYTES)


def _pick(n, pref, mult=LANES):
    if n <= pref:
        return n
    t = pref - pref % mult
    while t > mult and n % t:
        t -= mult
    assert n % t == 0, (n, pref, mult)
    return t


def _sigmoid(x):
    return 1.0 / (1.0 + jnp.exp(-x))


def _rms(x, eps=NORM_EPS):
    return x * lax.rsqrt(jnp.mean(x * x, axis=-1, keepdims=True) + eps)


def _dot(a, b):
    return jnp.dot(a, b, preferred_element_type=F32)


def _dot_nt(a, b):
    return lax.dot_general(a, b, (((1,), (1,)), ((), ())), preferred_element_type=F32)


def _dot_tn(a, b):
    return lax.dot_general(a, b, (((0,), (0,)), ((), ())), preferred_element_type=F32)


def _adaln_kernel(c_ref, w_ref, b_ref, o_ref):
    c = c_ref[...]
    a = (c * _sigmoid(c)).astype(BF16)
    o_ref[...] = _dot(a, w_ref[...].astype(BF16)) + b_ref[...]


def _adaln(c, w, b):
    n, d = c.shape
    nout = w.shape[1]
    tn = _pick(nout, 1024)
    return pl.pallas_call(
        _adaln_kernel,
        grid=(nout // tn,),
        in_specs=[pl.BlockSpec((n, d), lambda j: (0, 0)),
                  pl.BlockSpec((d, tn), lambda j: (0, j)),
                  pl.BlockSpec((1, tn), lambda j: (0, j))],
        out_specs=pl.BlockSpec((n, tn), lambda j: (0, j)),
        out_shape=jax.ShapeDtypeStruct((n, nout), F32),
        compiler_params=_cparams("parallel"),
        name="adaln",
    )(c, w, b.reshape(1, nout))


def _inproj_kernel(x_ref, sc_ref, sh_ref, nw_ref, wa_ref, wh_ref, qaw_ref, kvw_ref, wkr_ref, wkrs_ref,
                   cos_ref, sin_ref,
                   cqn_ref, rows_ref, latb_ref, krr_ref, krsq_ref, zh_ref, h_scr, *, q_lora, kv_lora, rope):
    j = pl.program_id(1)

    @pl.when(j == 0)
    def _():
        h = _rms(x_ref[...]) * nw_ref[...]
        h = h * (1.0 + sc_ref[...]) + sh_ref[...]
        hb = h.astype(BF16)
        h_scr[...] = hb
        za = _dot(hb, wa_ref[...])
        cqn_ref[...] = (_rms(za[:, :q_lora]) * qaw_ref[...]).astype(BF16)
        lat = _rms(za[:, q_lora:q_lora + kv_lora]) * kvw_ref[...]
        o = q_lora + kv_lora
        kr_a = za[:, o:o + LANES]
        kr_b = za[:, o + LANES:o + 2 * LANES]
        rows_ref[:, :kv_lora] = lat
        rows_ref[:, kv_lora:] = kr_a[:, :rope]
        latb_ref[...] = lat.astype(BF16)
        krr_ref[...] = kr_a * wkr_ref[...] * cos_ref[...] + kr_b * wkrs_ref[...] * sin_ref[...]
        krsq_ref[...] = jnp.broadcast_to(jnp.sum(kr_a * kr_a, axis=-1, keepdims=True), krsq_ref.shape)

    @pl.when(j > 0)
    def _():
        zh_ref[...] = _dot(h_scr[...], wh_ref[...])


def _inproj(x, sc, sh, mod_spec, nw, wa, wh, qaw, kvw, wkr, wkrs, cos_t, sin_t, tab_map, *, tm,
            q_lora, kv_lora, rope):
    t, d = x.shape
    na = wa.shape[1]
    nh = wh.shape[1]
    tn = _pick(nh, 1024)
    nj = nh // tn
    const = lambda i, j: (0, 0)
    row = lambda i, j: (i, 0)
    kern = functools.partial(_inproj_kernel, q_lora=q_lora, kv_lora=kv_lora, rope=rope)
    return pl.pallas_call(
        kern,
        grid=(t // tm, 1 + nj),
        in_specs=[pl.BlockSpec((tm, d), row), mod_spec, mod_spec,
                  pl.BlockSpec((1, d), const),
                  pl.BlockSpec((d, na), const),
                  pl.BlockSpec((d, tn), lambda i, j: (0, jnp.maximum(j - 1, 0))),
                  pl.BlockSpec((1, q_lora), const), pl.BlockSpec((1, kv_lora), const),
                  pl.BlockSpec((1, LANES), const), pl.BlockSpec((1, LANES), const),
                  pl.BlockSpec((tm, LANES), tab_map), pl.BlockSpec((tm, LANES), tab_map)],
        out_specs=[pl.BlockSpec((tm, q_lora), row),
                   pl.BlockSpec((tm, kv_lora + rope), row),
                   pl.BlockSpec((tm, kv_lora), row),
                   pl.BlockSpec((tm, LANES), row),
                   pl.BlockSpec((tm, LANES), row),
                   pl.BlockSpec((tm, tn), lambda i, j: (i, jnp.maximum(j - 1, 0)))],
        out_shape=[jax.ShapeDtypeStruct((t, q_lora), BF16),
                   jax.ShapeDtypeStruct((t, kv_lora + rope), F32),
                   jax.ShapeDtypeStruct((t, kv_lora), BF16),
                   jax.ShapeDtypeStruct((t, LANES), F32),
                   jax.ShapeDtypeStruct((t, LANES), F32),
                   jax.ShapeDtypeStruct((t, nh), F32)],
        scratch_shapes=[pltpu.VMEM((tm, d), BF16)],
        compiler_params=_cparams("parallel", "arbitrary"),
        name="inproj",
    )(x, sc, sh, nw, wa, wh, qaw, kvw, wkr, wkrs, cos_t, sin_t)


def _q_kernel(cqn_ref, w_ref, wn_ref, wa_ref, wb_ref, c1_ref, s1_ref, c2_ref, s2_ref, o_ref, *,
              scale, emit_rot, d_head):
    y = _dot(cqn_ref[...], w_ref[...])
    nope = y[:, :LANES]
    a = y[:, LANES:2 * LANES]
    b = y[:, 2 * LANES:]
    ssq = jnp.sum(nope * nope, axis=-1, keepdims=True) + jnp.sum(a * a, axis=-1, keepdims=True)
    rinv = lax.rsqrt(ssq * (1.0 / d_head) + NORM_EPS) * scale
    aw = a * wa_ref[...]
    bw = b * wb_ref[...]
    o_ref[:, :LANES] = (nope * rinv * wn_ref[...]).astype(o_ref.dtype)
    o_ref[:, LANES:2 * LANES] = ((aw * c1_ref[...] + bw * s1_ref[...]) * rinv).astype(o_ref.dtype)
    if emit_rot:
        o_ref[:, 2 * LANES:] = ((bw * c2_ref[...] + aw * s2_ref[...]) * rinv).astype(o_ref.dtype)


def _queries(cqn, wq, wn, wa, wb, tabs, tab_map, *, tm, scale, emit_rot, d_head):
    t, r = cqn.shape
    nheads = wq.shape[0]
    wout = 3 * LANES if emit_rot else 2 * LANES
    const = lambda i, h: (0, 0)
    kern = functools.partial(_q_kernel, scale=scale, emit_rot=emit_rot, d_head=d_head)
    tab_spec = pl.BlockSpec((tm, LANES), lambda i, h: tab_map(i, h))
    return pl.pallas_call(
        kern,
        grid=(t // tm, nheads),
        in_specs=[pl.BlockSpec((tm, r), lambda i, h: (i, 0)),
                  pl.BlockSpec((None, r, 3 * LANES), lambda i, h: (h, 0, 0)),
                  pl.BlockSpec((1, LANES), const), pl.BlockSpec((1, LANES), const),
                  pl.BlockSpec((1, LANES), const),
                  tab_spec, tab_spec, tab_spec, tab_spec],
        out_specs=pl.BlockSpec((None, tm, wout), lambda i, h: (h, i, 0)),
        out_shape=jax.ShapeDtypeStruct((nheads, t, wout), BF16),
        compiler_params=_cparams("parallel", "parallel"),
        name="queries",
    )(cqn, wq, wn, wa, wb, *tabs)


def _kv_kernel(latb_ref, krr_ref, krsq_ref, w_ref, wn_ref, k_ref, v_ref, *, d_head):
    y = _dot(latb_ref[...], w_ref[...])
    kn = y[:, :LANES]
    ssq = jnp.sum(kn * kn, axis=-1, keepdims=True) + krsq_ref[:, :1]
    rinv = lax.rsqrt(ssq * (1.0 / d_head) + NORM_EPS)
    k_ref[:, :LANES] = (kn * rinv * wn_ref[...]).astype(BF16)
    k_ref[:, LANES:] = (krr_ref[...] * rinv).astype(BF16)
    v_ref[...] = y[:, LANES:].astype(BF16)


def _keys_values(latb, krr, krsq, wkv, wn, *, tm, d_head):
    t, c = latb.shape
    nheads = wkv.shape[0]
    row = lambda i, h: (i, 0)
    return pl.pallas_call(
        functools.partial(_kv_kernel, d_head=d_head),
        grid=(t // tm, nheads),
        in_specs=[pl.BlockSpec((tm, c), row), pl.BlockSpec((tm, LANES), row), pl.BlockSpec((tm, LANES), row),
                  pl.BlockSpec((None, c, 2 * LANES), lambda i, h: (h, 0, 0)),
                  pl.BlockSpec((1, LANES), lambda i, h: (0, 0))],
        out_specs=[pl.BlockSpec((None, tm, 2 * LANES), lambda i, h: (h, i, 0)),
                   pl.BlockSpec((None, tm, LANES), lambda i, h: (h, i, 0))],
        out_shape=[jax.ShapeDtypeStruct((nheads, t, 2 * LANES), BF16),
                   jax.ShapeDtypeStruct((nheads, t, LANES), BF16)],
        compiler_params=_cparams("parallel", "parallel"),
        name="keys_values",
    )(latb, krr, krsq, wkv, wn)


def _flash_kernel(q_ref, k_ref, v_ref, o_ref, m_scr, l_scr, acc_scr, *, tq):
    qi = pl.program_id(2)
    ki = pl.program_id(3)

    @pl.when(ki == 0)
    def _():
        m_scr[...] = jnp.full(m_scr.shape, NEG_INF, F32)
        l_scr[...] = jnp.zeros(l_scr.shape, F32)
        acc_scr[...] = jnp.zeros(acc_scr.shape, F32)

    @pl.when(ki <= qi)
    def _():
        s = _dot_nt(q_ref[...], k_ref[...])
        qpos = qi * tq + lax.broadcasted_iota(jnp.int32, s.shape, 0)
        kpos = ki * tq + lax.broadcasted_iota(jnp.int32, s.shape, 1)
        s = jnp.where(kpos <= qpos, s, NEG_INF)
        m_old = m_scr[...]
        m_new = jnp.maximum(m_old, jnp.max(s, axis=-1, keepdims=True))
        alpha = jnp.exp(m_old - m_new)
        p = jnp.exp(s - m_new)
        l_scr[...] = alpha * l_scr[...] + jnp.sum(p, axis=-1, keepdims=True)
        acc_scr[...] = alpha * acc_scr[...] + _dot(p.astype(BF16), v_ref[...])
        m_scr[...] = m_new

    @pl.when(ki == pl.num_programs(3) - 1)
    def _():
        o_ref[...] = acc_scr[...] / l_scr[...]


def _flash(q, k, v, *, nbatch, seq, tq):
    nheads, t, dq = q.shape
    dv = v.shape[-1]
    nq = seq // tq
    return pl.pallas_call(
        functools.partial(_flash_kernel, tq=tq),
        grid=(nbatch, nheads, nq, nq),
        in_specs=[pl.BlockSpec((None, tq, 2 * LANES), lambda b, h, qi, ki: (h, b * nq + qi, 0)),
                  pl.BlockSpec((None, tq, dq), lambda b, h, qi, ki: (h, b * nq + jnp.minimum(ki, qi), 0)),
                  pl.BlockSpec((None, tq, dv), lambda b, h, qi, ki: (h, b * nq + jnp.minimum(ki, qi), 0))],
        out_specs=pl.BlockSpec((tq, dv), lambda b, h, qi, ki: (b * nq + qi, h)),
        out_shape=jax.ShapeDtypeStruct((t, nheads * dv), F32),
        scratch_shapes=[pltpu.VMEM((tq, 1), F32), pltpu.VMEM((tq, 1), F32), pltpu.VMEM((tq, dv), F32)],
        compiler_params=_cparams("parallel", "parallel", "parallel", "arbitrary"),
        name="prompt_attention",
    )(q, k, v)


def _absorb_kernel(q_ref, wuk_ref, wkn_ref, wkr_ref, wkrs_ref, o_ref, *, kv_lora):
    q = q_ref[...].astype(F32)
    qn = (q[:, :LANES] * wkn_ref[...]).astype(BF16)
    o_ref[:, :kv_lora] = _dot_nt(qn, wuk_ref[...]).astype(BF16)
    o_ref[:, kv_lora:kv_lora + LANES] = (q[:, LANES:2 * LANES] * wkr_ref[...]).astype(BF16)
    o_ref[:, kv_lora + LANES:] = (q[:, 2 * LANES:] * wkr_ref[...]).astype(BF16)


def _absorb(q, wuk, wkn, wkr, *, kv_lora):
    nheads, t, wq = q.shape
    wout = kv_lora + 2 * LANES
    const = lambda h: (0, 0)
    return pl.pallas_call(
        functools.partial(_absorb_kernel, kv_lora=kv_lora),
        grid=(nheads,),
        in_specs=[pl.BlockSpec((None, t, wq), lambda h: (h, 0, 0)),
                  pl.BlockSpec((None, kv_lora, LANES), lambda h: (h, 0, 0)),
                  pl.BlockSpec((1, LANES), const), pl.BlockSpec((1, LANES), const),
                  pl.BlockSpec((1, LANES), const)],
        out_specs=pl.BlockSpec((t, wout), lambda h: (0, h)),
        out_shape=jax.ShapeDtypeStruct((t, nheads * wout), BF16),
        compiler_params=_cparams("parallel"),
        name="absorb_queries",
    )(q, wuk, wkn, wkr, wkr)


def _sattn_kernel(pt_ref, *refs, n_group, kv_lora, rope, nheads, ntok, d_head, page):
    page_refs = refs[:n_group]
    (new_ref, qf_ref, wukt_ref, cos_ref, sin_ref, cosn_ref, sinn_ref,
     o_ref, lhs_scr, m_scr, l_scr, acc_scr) = refs[n_group:]
    step = pl.program_id(1)
    nrow = nheads * ntok
    nk_w = wukt_ref.shape[0]

    @pl.when(step == 0)
    def _():
        lhs_scr[:nk_w, :] = wukt_ref[...]
        lhs_scr[nk_w:, :] = qf_ref[:, :kv_lora]
        m_scr[...] = jnp.full(m_scr.shape, NEG_INF, F32)
        l_scr[...] = jnp.zeros(l_scr.shape, F32)
        acc_scr[...] = jnp.zeros(acc_scr.shape, F32)

    def process(rows, cosb, sinb, mask):
        nk = rows.shape[0]
        lat = rows[:, :kv_lora].astype(BF16)
        kr = rows[:, kv_lora:kv_lora + rope]
        r = _dot_nt(lhs_scr[...], lat)
        kn = r[:nk_w]
        kn2 = jnp.sum((kn * kn).reshape(nheads, nk_w // nheads, nk), axis=1)
        krsq = _dot_nt(jnp.ones((SUBLANES, rope), F32), kr * kr)
        ssq = kn2 + krsq[:nheads] if nheads <= SUBLANES else kn2 + krsq[:1]
        rinv = lax.rsqrt(ssq * (1.0 / d_head) + NORM_EPS)
        rinv_rows = jnp.concatenate([rinv] * ntok, axis=0)
        qrc = qf_ref[:, kv_lora:kv_lora + rope]
        qrs = qf_ref[:, kv_lora + LANES:kv_lora + LANES + rope]
        s_rope = _dot_nt(qrc, (kr * cosb).astype(BF16)) + _dot_nt(qrs, (kr * sinb).astype(BF16))
        s = (r[nk_w:] + s_rope) * rinv_rows
        if mask is not None:
            s = jnp.where(mask, s, NEG_INF)
        m_old = m_scr[...]
        m_new = jnp.maximum(m_old, jnp.max(s, axis=-1, keepdims=True))
        alpha = jnp.exp(m_old - m_new)
        p = jnp.exp(s - m_new)
        l_scr[...] = alpha * l_scr[...] + jnp.sum(p, axis=-1, keepdims=True)
        acc_scr[...] = alpha * acc_scr[...] + _dot(p.astype(BF16), lat)
        m_scr[...] = m_new

    per = 2 if n_group % 2 == 0 else 1
    for g in range(0, n_group, per):
        rows = jnp.concatenate([page_refs[g + u][...] for u in range(per)], axis=0)
        process(rows, cos_ref[g * page:(g + per) * page, :], sin_ref[g * page:(g + per) * page, :], None)

    @pl.when(step == pl.num_programs(1) - 1)
    def _():
        key = lax.broadcasted_iota(jnp.int32, (nrow, page), 1)
        tok = lax.broadcasted_iota(jnp.int32, (nrow, page), 0) // nheads
        process(new_ref[...], cosn_ref[...], sinn_ref[...], key <= tok)
        o_ref[...] = acc_scr[...] / l_scr[...]


def _sample_attention(page_table, cache, new_pages, qf, wukt, cos_p, sin_p, cos_n, sin_n, *,
                      n_group, kv_lora, rope, nheads, ntok, d_head):
    nb, n_pages = page_table.shape
    page = cache.shape[2]
    cw = cache.shape[3]
    nrow = nheads * ntok
    wqf = qf.shape[-1]
    nsteps = n_pages // n_group

    def page_map(g):
        return lambda b, s, pt: (0, pt[b * n_pages + s * n_group + g], 0, 0)

    kern = functools.partial(_sattn_kernel, n_group=n_group, kv_lora=kv_lora, rope=rope, nheads=nheads,
                             ntok=ntok, d_head=d_head, page=page)
    in_specs = [pl.BlockSpec((None, None, page, cw), page_map(g)) for g in range(n_group)]
    in_specs += [pl.BlockSpec((None, page, cw), lambda b, s, pt: (b, 0, 0)),
                 pl.BlockSpec((None, nrow, wqf), lambda b, s, pt: (b, 0, 0)),
                 pl.BlockSpec(wukt.shape, lambda b, s, pt: (0, 0)),
                 pl.BlockSpec((n_group * page, rope), lambda b, s, pt: (s, 0)),
                 pl.BlockSpec((n_group * page, rope), lambda b, s, pt: (s, 0)),
                 pl.BlockSpec((page, rope), lambda b, s, pt: (0, 0)),
                 pl.BlockSpec((page, rope), lambda b, s, pt: (0, 0))]
    grid_spec = pltpu.PrefetchScalarGridSpec(
        num_scalar_prefetch=1,
        grid=(nb, nsteps),
        in_specs=in_specs,
        out_specs=pl.BlockSpec((None, nrow, kv_lora), lambda b, s, pt: (b, 0, 0)),
        scratch_shapes=[pltpu.VMEM((wukt.shape[0] + nrow, kv_lora), BF16),
                        pltpu.VMEM((nrow, 1), F32), pltpu.VMEM((nrow, 1), F32),
                        pltpu.VMEM((nrow, kv_lora), F32)])
    return pl.pallas_call(
        kern,
        grid_spec=grid_spec,
        out_shape=jax.ShapeDtypeStruct((nb, nrow, kv_lora), F32),
        compiler_params=_cparams("parallel", "arbitrary"),
        name="sample_attention",
    )(page_table.reshape(-1), *([cache] * n_group), new_pages, qf, wukt, cos_p, sin_p, cos_n, sin_n)


def _uv_kernel(lm_ref, w_ref, o_ref):
    o_ref[...] = _dot(lm_ref[...].astype(BF16), w_ref[...])


def _uv_project(lm, wuv):
    t = lm.shape[0]
    nheads, c, dv = wuv.shape
    return pl.pallas_call(
        _uv_kernel,
        grid=(nheads,),
        in_specs=[pl.BlockSpec((t, c), lambda h: (0, h)),
                  pl.BlockSpec((None, c, dv), lambda h: (h, 0, 0))],
        out_specs=pl.BlockSpec((t, dv), lambda h: (0, h)),
        out_shape=jax.ShapeDtypeStruct((t, nheads * dv), F32),
        compiler_params=_cparams("parallel"),
        name="value_up_projection",
    )(lm, wuv)


def _cumsum_rows(g, chunk):
    rows = lax.broadcasted_iota(jnp.int32, (chunk, chunk), 0)
    cols = lax.broadcasted_iota(jnp.int32, (chunk, chunk), 1)
    if chunk <= SUBLANES:
        ridx = lax.broadcasted_iota(jnp.int32, g.shape, 0)
        out = jnp.zeros_like(g)
        for s in range(chunk):
            out = out + jnp.where(ridx >= s, g[s:s + 1, :], 0.0)
        return out
    tri = (cols <= rows).astype(BF16)
    hi = g.astype(BF16)
    r1 = g - hi.astype(F32)
    mid = r1.astype(BF16)
    lo = (r1 - mid.astype(F32)).astype(BF16)
    return _dot(tri, hi) + (_dot(tri, mid) + _dot(tri, lo))


def _hgrn_kernel(hq_ref, hf_ref, hi_ref, hg_ref, lbp_ref, s0_ref, gnw_ref, o_ref, sout_ref, st_scr, *,
                 chunk, sub, nchunk, t_valid, has_s0):
    tb = pl.program_id(2)

    @pl.when(tb == 0)
    def _():
        if has_s0:
            st_scr[...] = s0_ref[...].T
        else:
            st_scr[...] = jnp.zeros(st_scr.shape, F32)

    a = lbp_ref[...]
    e = jnp.exp(a - jnp.max(a, axis=0, keepdims=True))
    lb = e[0:1, :] / jnp.sum(e, axis=0, keepdims=True)

    st = st_scr[...]
    gnw = gnw_ref[...]
    for c in range(nchunk):
        sl = slice(c * chunk, (c + 1) * chunk)
        zq = hq_ref[sl, :]
        q = zq * _sigmoid(zq)
        k = (1.0 - lb) * _sigmoid(-hf_ref[sl, :])
        g = jnp.log(1.0 - k)
        if t_valid is not None:
            valid = (tb * (nchunk * chunk) + c * chunk
                     + lax.broadcasted_iota(jnp.int32, k.shape, 0)) < t_valid
            k = jnp.where(valid, k, 0.0)
            g = jnp.where(valid, g, 0.0)
        v = hi_ref[sl, :]
        zg = hg_ref[sl, :]
        gate = zg * _sigmoid(zg)

        gc = _cumsum_rows(g, chunk)
        o = _dot_nt(q * jnp.exp(gc), st)
        g_last = gc[chunk - 1:chunk, :]
        u_t = _dot_tn(v, k * jnp.exp(g_last - gc))

        parts = []
        for i in range(chunk // sub):
            r0 = i * sub
            gi = gc[r0:r0 + sub, :]
            qi = q[r0:r0 + sub, :]
            if i > 0:
                ref_row = gc[r0 - 1:r0, :]
                a_off = _dot_nt(qi * jnp.exp(gi - ref_row), k[:r0, :] * jnp.exp(ref_row - gc[:r0, :]))
                o_i = _dot(a_off, v[:r0, :])
            else:
                o_i = jnp.zeros((sub, v.shape[1]), F32)
            trow = lax.broadcasted_iota(jnp.int32, (sub, 1), 0)
            for s in range(sub):
                keep = trow >= s
                d = jnp.where(keep, gi - gc[r0 + s:r0 + s + 1, :], 0.0)
                w = jnp.exp(d) * qi * k[r0 + s:r0 + s + 1, :]
                a_col = jnp.where(keep, jnp.sum(w, axis=-1, keepdims=True), 0.0)
                o_i = o_i + a_col * v[r0 + s:r0 + s + 1, :]
            parts.append(o_i)
        o = o + (parts[0] if len(parts) == 1 else jnp.concatenate(parts, axis=0))
        st = st * jnp.exp(g_last) + u_t
        o_ref[sl, :] = _rms(o) * gnw * gate

    st_scr[...] = st

    @pl.when(tb == pl.num_programs(2) - 1)
    def _():
        sout_ref[...] = st.T


def _hgrn(zh, lbp, s0, gnw, *, nheads, tblock, chunk, sub, t_valid):
    n, t, _ = zh.shape
    dk = LANES
    has_s0 = s0 is not None
    nlb = lbp.shape[0]
    if not has_s0:
        s0 = jnp.zeros((1, 1, dk, dk), F32)
        s0_spec = pl.BlockSpec((None, None, dk, dk), lambda b, h, tb: (0, 0, 0, 0))
    else:
        s0_spec = pl.BlockSpec((None, None, dk, dk), lambda b, h, tb: (b, h, 0, 0))

    def col(group):
        return pl.BlockSpec((None, tblock, dk), lambda b, h, tb: (b, tb, group * nheads + h))

    kern = functools.partial(_hgrn_kernel, chunk=chunk, sub=sub, nchunk=tblock // chunk, t_valid=t_valid,
                             has_s0=has_s0)
    return pl.pallas_call(
        kern,
        grid=(n, nheads, t // tblock),
        in_specs=[col(0), col(1), col(2), col(3),
                  pl.BlockSpec((nlb, dk), lambda b, h, tb: (0, h)),
                  s0_spec,
                  pl.BlockSpec((1, dk), lambda b, h, tb: (0, 0))],
        out_specs=[pl.BlockSpec((None, tblock, dk), lambda b, h, tb: (b, tb, h)),
                   pl.BlockSpec((None, None, dk, dk), lambda b, h, tb: (b, h, 0, 0))],
        out_shape=[jax.ShapeDtypeStruct((n, t, nheads * dk), F32),
                   jax.ShapeDtypeStruct((n, nheads, dk, dk), F32)],
        scratch_shapes=[pltpu.VMEM((dk, dk), F32)],
        compiler_params=_cparams("parallel", "parallel", "arbitrary"),
        name="hgrn2",
    )(zh, zh, zh, zh, lbp, s0, gnw)


def _outproj_kernel(oa_ref, orec_ref, x_ref, ga_ref, scf_ref, shf_ref, gnw_ref, fnw_ref, wo_ref, rw_ref,
                    rb_ref, x1_ref, h2_ref, topi_ref, gate_ref, *, n_attn):
    oan = (_rms(oa_ref[...]) * gnw_ref[...]).astype(BF16)
    mix = _dot(oan, wo_ref[:n_attn, :]) + _dot(orec_ref[...].astype(BF16), wo_ref[n_attn:, :])
    x1 = x_ref[...] + ga_ref[...] * mix
    x1_ref[...] = x1
    h2 = _rms(x1) * fnw_ref[...] * (1.0 + scf_ref[...]) + shf_ref[...]
    h2_ref[...] = h2
    logits = jnp.dot(h2, rw_ref[...], preferred_element_type=F32,
                     precision=lax.Precision.HIGHEST) + rb_ref[...]
    lane = lax.broadcasted_iota(jnp.int32, logits.shape, 1).astype(F32)
    vals = []
    topi = jnp.zeros(logits.shape, F32)
    work = logits
    for kk in range(TOP_K):
        m = jnp.max(work, axis=-1, keepdims=True)
        idx = jnp.min(jnp.where(work == m, lane, float(LANES)), axis=-1, keepdims=True)
        vals.append(m)
        topi = jnp.where(lane == float(kk), idx, topi)
        work = jnp.where(lane == idx, -3e38, work)
    es = [jnp.exp(vv - vals[0]) for vv in vals]
    den = es[0]
    for ee in es[1:]:
        den = den + ee
    gates = jnp.zeros(logits.shape, F32)
    for kk in range(TOP_K):
        gates = jnp.where(lane == float(kk), es[kk] / den, gates)
    topi_ref[...] = topi.astype(jnp.int32)
    gate_ref[...] = gates


def _outproj(oa, orec, x, ga, scf, shf, mod_spec, gnw, fnw, wo, rw, rb, *, tm):
    t, d = x.shape
    n_attn = oa.shape[1]
    n_rec = orec.shape[1]
    row = lambda i: (i, 0)
    const = lambda i: (0, 0)
    return pl.pallas_call(
        functools.partial(_outproj_kernel, n_attn=n_attn),
        grid=(t // tm,),
        in_specs=[pl.BlockSpec((tm, n_attn), row), pl.BlockSpec((tm, n_rec), row), pl.BlockSpec((tm, d), row),
                  mod_spec, mod_spec, mod_spec,
                  pl.BlockSpec((1, n_attn), const), pl.BlockSpec((1, d), const),
                  pl.BlockSpec(wo.shape, const), pl.BlockSpec(rw.shape, const), pl.BlockSpec((1, LANES), const)],
        out_specs=[pl.BlockSpec((tm, d), row), pl.BlockSpec((tm, d), row),
                   pl.BlockSpec((tm, LANES), row), pl.BlockSpec((tm, LANES), row)],
        out_shape=[jax.ShapeDtypeStruct((t, d), F32), jax.ShapeDtypeStruct((t, d), F32),
                   jax.ShapeDtypeStruct((t, LANES), jnp.int32), jax.ShapeDtypeStruct((t, LANES), F32)],
        compiler_params=_cparams("parallel"),
        name="outproj_router",
    )(oa, orec, x, ga, scf, shf, gnw, fnw, wo, rw, rb)


def _gather_kernel(idx_ref, src_ref, dst_ref, sem, *, rows_per_step):
    base = pl.program_id(0) * rows_per_step

    def row_copy(r, src_row):
        return pltpu.make_async_copy(src_ref.at[pl.ds(src_row, 1), :], dst_ref.at[pl.ds(base + r, 1), :], sem)

    def issue(r, carry):
        row_copy(r, idx_ref[r]).start()
        return carry

    def drain(r, carry):
        row_copy(r, 0).wait()
        return carry

    lax.fori_loop(0, rows_per_step, issue, 0)
    lax.fori_loop(0, rows_per_step, drain, 0)


def _gather_rows(idx, src, *, rows_per_step):
    n = idx.shape[0]
    return pl.pallas_call(
        functools.partial(_gather_kernel, rows_per_step=rows_per_step),
        grid=(n // rows_per_step,),
        in_specs=[pl.BlockSpec((rows_per_step,), lambda i: (i,), memory_space=pltpu.SMEM),
                  pl.BlockSpec(memory_space=pl.ANY)],
        out_specs=pl.BlockSpec(memory_space=pl.ANY),
        out_shape=jax.ShapeDtypeStruct((n, src.shape[1]), src.dtype),
        scratch_shapes=[pltpu.SemaphoreType.DMA(())],
        compiler_params=_cparams("arbitrary"),
        name="gather_rows",
    )(idx, src)


def _moe_kernel(te_ref, nu_ref, x_ref, w1g_ref, w1l_ref, b1g_ref, b1l_ref, w2_ref, b2_ref, o_ref,
                xb_scr, acc_scr):
    i = pl.program_id(0)
    j = pl.program_id(1)

    @pl.when(i < nu_ref[0])
    def _():
        @pl.when(j == 0)
        def _():
            xb_scr[...] = x_ref[...].astype(BF16)
            acc_scr[...] = jnp.zeros(acc_scr.shape, F32)

        xb = xb_scr[...]
        yg = _dot(xb, w1g_ref[...].astype(BF16)) + b1g_ref[...]
        yl = _dot(xb, w1l_ref[...].astype(BF16)) + b1l_ref[...]
        glu = jnp.minimum(yg, SWIGLU_LIMIT)
        lin = jnp.clip(yl, -SWIGLU_LIMIT, SWIGLU_LIMIT)
        a = glu * _sigmoid(SWIGLU_ALPHA * glu) * (lin + 1.0)
        acc_scr[...] += _dot(a.astype(BF16), w2_ref[...].astype(BF16))

        @pl.when(j == pl.num_programs(1) - 1)
        def _():
            o_ref[...] = acc_scr[...] + b2_ref[...]

    @pl.when(jnp.logical_and(i >= nu_ref[0], j == 0))
    def _():
        o_ref[...] = jnp.zeros(o_ref.shape, F32)


def _moe_ffn(tile_e, n_used, xs, w1, b1, w2, b2, *, tm, th):
    p, d = xs.shape
    ne, _, two_de = w1.shape
    de = two_de // 2
    nj = de // th
    ntiles = p // tm

    def tile(i, nu):
        return jnp.minimum(i, nu[0] - 1)

    def hid(i, j, nu):
        return jnp.where(i < nu[0], j, nj - 1)

    grid_spec = pltpu.PrefetchScalarGridSpec(
        num_scalar_prefetch=2,
        grid=(ntiles, nj),
        in_specs=[pl.BlockSpec((tm, d), lambda i, j, te, nu: (tile(i, nu), 0)),
                  pl.BlockSpec((None, d, th), lambda i, j, te, nu: (te[tile(i, nu)], 0, hid(i, j, nu))),
                  pl.BlockSpec((None, d, th), lambda i, j, te, nu: (te[tile(i, nu)], 0, nj + hid(i, j, nu))),
                  pl.BlockSpec((None, 1, th), lambda i, j, te, nu: (te[tile(i, nu)], 0, hid(i, j, nu))),
                  pl.BlockSpec((None, 1, th), lambda i, j, te, nu: (te[tile(i, nu)], 0, nj + hid(i, j, nu))),
                  pl.BlockSpec((None, th, d), lambda i, j, te, nu: (te[tile(i, nu)], hid(i, j, nu), 0)),
                  pl.BlockSpec((None, 1, d), lambda i, j, te, nu: (te[tile(i, nu)], 0, 0))],
        out_specs=pl.BlockSpec((tm, d), lambda i, j, te, nu: (i, 0)),
        scratch_shapes=[pltpu.VMEM((tm, d), BF16), pltpu.VMEM((tm, d), F32)])
    return pl.pallas_call(
        _moe_kernel,
        grid_spec=grid_spec,
        out_shape=jax.ShapeDtypeStruct((p, d), F32),
        compiler_params=_cparams("arbitrary", "arbitrary"),
        name="moe_experts",
    )(tile_e, n_used, xs, w1, w1, b1.reshape(ne, 1, two_de), b1.reshape(ne, 1, two_de), w2,
      b2.reshape(ne, 1, d))


def _combine_kernel(yg_ref, gate_ref, x1_ref, gf_ref, o_ref, *, d):
    gates = gate_ref[...]
    acc = gates[:, 0:1] * yg_ref[:, :d]
    for kk in range(1, TOP_K):
        acc = acc + gates[:, kk:kk + 1] * yg_ref[:, kk * d:(kk + 1) * d]
    o_ref[...] = x1_ref[...] + gf_ref[...] * acc


def _combine(yg, gates, x1, gf, mod_spec, *, tm, row_off):
    t, d = x1.shape
    return pl.pallas_call(
        functools.partial(_combine_kernel, d=d),
        grid=(t // tm,),
        in_specs=[pl.BlockSpec((tm, TOP_K * d), lambda i: (i + row_off, 0)),
                  pl.BlockSpec((tm, LANES), lambda i: (i + row_off, 0)),
                  pl.BlockSpec((tm, d), lambda i: (i, 0)),
                  mod_spec],
        out_specs=pl.BlockSpec((tm, d), lambda i: (i, 0)),
        out_shape=jax.ShapeDtypeStruct((t, d), F32),
        compiler_params=_cparams("parallel"),
        name="moe_combine",
    )(yg, gates, x1, gf)


def _rope_tables(pos, half):
    inv = ROPE_THETA ** (-jnp.arange(half, dtype=F32) / half)
    ang = pos.astype(F32)[:, None] * inv[None, :]
    cos, sin = jnp.cos(ang), jnp.sin(ang)
    pad = jnp.zeros((pos.shape[0], LANES - 2 * half), F32)
    c1 = jnp.concatenate([cos, cos, pad], -1)
    s1 = jnp.concatenate([-sin, sin, pad], -1)
    c2 = jnp.concatenate([cos, -cos, pad], -1)
    s2 = jnp.concatenate([sin, sin, pad], -1)
    return c1, s1, c2, s2


def _pad_lanes(v, width=LANES):
    v = v.reshape(1, -1).astype(F32)
    return jnp.pad(v, ((0, 0), (0, width - v.shape[1])))


def _swap_halves(v):
    h = v.shape[-1] // 2
    return jnp.concatenate([v[..., h:], v[..., :h]], -1)


def kernel(x_prompt, x_sample, cache_mla, state_hgrn, page_table, c_prompt, c_sample, w_ada, b_ada,
           attn_norm_w, w_in, q_a_norm_w, w_uq, kv_a_norm_w, w_ukv, q_head_norm_w, k_head_norm_w,
           attn_group_norm_w, hg_lower_bound, hg_norm_w, w_out, ffn_norm_w, router_w, router_b,
           w1, b1, w2, b2):
    nb_p, seq, d = x_prompt.shape
    nb_s, ntok, _ = x_sample.shape
    depth = w_ada.shape[0]
    assert depth == 1, "single-layer step"
    page = cache_mla.shape[2]
    cache_w = cache_mla.shape[3]
    n_pages = page_table.shape[1]
    past_len = n_pages * page
    q_lora, nheads, d_head = w_uq.shape[1], w_uq.shape[2], w_uq.shape[3]
    kv_lora = w_ukv.shape[1]
    rope = cache_w - kv_lora
    nope = d_head - rope
    v_head = w_ukv.shape[3] - nope
    hg_heads, hg_dk, hg_dv = state_hgrn.shape[2], state_hgrn.shape[3], state_hgrn.shape[4]
    n_experts = router_w.shape[2]
    assert nope == LANES and v_head == LANES and hg_dk == LANES and hg_dv == LANES
    assert 2 * rope <= LANES and n_experts <= LANES
    half = rope // 2
    attn_scale = d_head ** -0.5
    t_p = nb_p * seq
    t_s = nb_s * ntok

    mod = _adaln(jnp.concatenate([c_prompt, c_sample], 0), w_ada[0], b_ada[0])
    mods = [mod[:, i * d:(i + 1) * d] for i in range(6)]
    mods_p = [m[:nb_p].reshape(nb_p, 1, d) for m in mods]
    mods_s = [jnp.repeat(m[nb_p:], ntok, axis=0) for m in mods]

    wi = w_in[0]
    o_kr = q_lora + kv_lora
    w_kr = wi[:, o_kr:o_kr + rope]
    zpad = jnp.zeros((d, LANES - rope), F32)
    wa = jnp.concatenate([wi[:, :o_kr], w_kr, zpad, _swap_halves(w_kr), zpad], 1).astype(BF16)
    wh = wi[:, o_kr + rope:].astype(BF16)
    wq_full = jnp.transpose(w_uq[0], (1, 0, 2))
    wq_r = wq_full[:, :, nope:]
    zq = jnp.zeros((nheads, q_lora, LANES - rope), F32)
    wq = jnp.concatenate([wq_full[:, :, :nope], wq_r, zq, _swap_halves(wq_r), zq], -1).astype(BF16)
    wkv = jnp.transpose(w_ukv[0], (1, 0, 2)).astype(BF16)
    wuk = wkv[:, :, :nope]
    wuv = wkv[:, :, nope:]
    wukt = jnp.transpose(wuk, (0, 2, 1)).reshape(nheads * nope, kv_lora)
    qn_w, kn_w = q_head_norm_w[0], k_head_norm_w[0]
    qw_n, qw_a, qw_b = _pad_lanes(qn_w[:nope]), _pad_lanes(qn_w[nope:]), _pad_lanes(_swap_halves(qn_w[nope:]))
    kw_n, kw_a, kw_b = _pad_lanes(kn_w[:nope]), _pad_lanes(kn_w[nope:]), _pad_lanes(_swap_halves(kn_w[nope:]))
    nw = attn_norm_w[0].reshape(1, d)
    qaw = q_a_norm_w[0].reshape(1, q_lora)
    kvw = kv_a_norm_w[0].reshape(1, kv_lora)

    tabs_p = _rope_tables(jnp.arange(seq), half)
    tabs_s = _rope_tables(past_len + (jnp.arange(t_s) % ntok), half)

    tm_p = min(512, seq)
    tm_s = t_s
    nblk = seq // tm_p
    modspec_p2 = pl.BlockSpec((None, 1, d), lambda i, j: (i // nblk, 0, 0))
    modspec_s2 = pl.BlockSpec((tm_s, d), lambda i, j: (i, 0))
    dims = dict(q_lora=q_lora, kv_lora=kv_lora, rope=rope)

    xp = x_prompt.reshape(t_p, d)
    xs = x_sample.reshape(t_s, d)
    cqn_p, rows_p, latb_p, krr_p, krsq_p, zh_p = _inproj(
        xp, mods_p[1], mods_p[0], modspec_p2, nw, wa, wh, qaw, kvw, kw_a, kw_b, tabs_p[0], tabs_p[1],
        lambda i, j: (i % nblk, 0), tm=tm_p, **dims)
    cqn_s, rows_s, latb_s, krr_s, krsq_s, zh_s = _inproj(
        xs, mods_s[1], mods_s[0], modspec_s2, nw, wa, wh, qaw, kvw, kw_a, kw_b, tabs_s[0], tabs_s[1],
        lambda i, j: (i, 0), tm=tm_s, **dims)

    q_p = _queries(cqn_p, wq, qw_n, qw_a, qw_b, tabs_p, lambda i, h: (i % nblk, 0), tm=tm_p,
                   scale=attn_scale, emit_rot=False, d_head=d_head)
    k_p, v_p = _keys_values(latb_p, krr_p, krsq_p, wkv, kw_n, tm=tm_p, d_head=d_head)
    oa_p = _flash(q_p, k_p, v_p, nbatch=nb_p, seq=seq, tq=tm_p)

    q_s = _queries(cqn_s, wq, qw_n, qw_a, qw_b, tabs_s, lambda i, h: (i, 0), tm=tm_s,
                   scale=attn_scale, emit_rot=True, d_head=d_head)
    qf = _absorb(q_s, wuk, kw_n, kw_a, kv_lora=kv_lora)
    qf = qf.reshape(nb_s, ntok * nheads, kv_lora + 2 * LANES)
    pos_tab = jnp.arange(past_len)
    inv = ROPE_THETA ** (-jnp.arange(half, dtype=F32) / half)

    def cs(pos):
        ang = pos.astype(F32)[:, None] * inv[None, :]
        return (jnp.concatenate([jnp.cos(ang)] * 2, -1), jnp.concatenate([jnp.sin(ang)] * 2, -1))

    cos_pg, sin_pg = cs(pos_tab)
    cos_n, sin_n = cs(past_len + jnp.arange(page))
    new_pages = jnp.zeros((nb_s, page, cache_w), F32).at[:, :ntok].set(rows_s.reshape(nb_s, ntok, cache_w))
    n_group = 8
    while n_pages % n_group:
        n_group //= 2
    lat_mix = _sample_attention(page_table, cache_mla, new_pages, qf, wukt, cos_pg, sin_pg, cos_n, sin_n,
                                n_group=n_group, kv_lora=kv_lora, rope=rope, nheads=nheads, ntok=ntok,
                                d_head=d_head)
    oa_s = _uv_project(lat_mix.reshape(t_s, nheads * kv_lora), wuv)

    gnw_h = hg_norm_w[0].reshape(1, hg_dv)
    chunk_p = min(HG_CHUNK, seq)
    orec_p, st_p = _hgrn(zh_p.reshape(nb_p, seq, -1), hg_lower_bound, None, gnw_h, nheads=hg_heads,
                         tblock=min(256, seq), chunk=chunk_p, sub=min(HG_SUB, chunk_p), t_valid=None)
    t_pad = -(-ntok // SUBLANES) * SUBLANES
    zh_s3 = jnp.pad(zh_s.reshape(nb_s, ntok, -1), ((0, 0), (0, t_pad - ntok), (0, 0)))
    orec_s, st_s = _hgrn(zh_s3, hg_lower_bound, state_hgrn[0], gnw_h, nheads=hg_heads, tblock=t_pad,
                         chunk=t_pad, sub=t_pad, t_valid=ntok)
    orec_s = orec_s[:, :ntok].reshape(t_s, -1)

    wo = w_out[0].astype(BF16)
    rw = jnp.pad(router_w[0], ((0, 0), (0, LANES - n_experts)))
    rb = jnp.concatenate([router_b[0], jnp.full((LANES - n_experts,), NEG_INF, F32)]).reshape(1, LANES)
    gnw_a = attn_group_norm_w[0].reshape(1, -1)
    fnw = ffn_norm_w[0].reshape(1, d)
    modspec_p1 = pl.BlockSpec((None, 1, d), lambda i: (i // nblk, 0, 0))
    modspec_s1 = pl.BlockSpec((tm_s, d), lambda i: (i, 0))
    x1_p, h2_p, ti_p, gt_p = _outproj(oa_p, orec_p.reshape(t_p, -1), xp, mods_p[2], mods_p[4], mods_p[3],
                                      modspec_p1, gnw_a, fnw, wo, rw, rb, tm=tm_p)
    x1_s, h2_s, ti_s, gt_s = _outproj(oa_s, orec_s, xs, mods_s[2], mods_s[4], mods_s[3],
                                      modspec_s1, gnw_a, fnw, wo, rw, rb, tm=tm_s)

    n_tok = t_p + t_s
    h2 = jnp.concatenate([h2_p, h2_s], 0)
    top_i = jnp.concatenate([ti_p[:, :TOP_K], ti_s[:, :TOP_K]], 0)
    gates = jnp.concatenate([gt_p, gt_s], 0)
    tm_e = min(512, n_tok)
    n_assign = n_tok * TOP_K
    flat_e = top_i.reshape(-1)
    onehot = (flat_e[:, None] == jnp.arange(n_experts, dtype=jnp.int32)[None, :]).astype(jnp.int32)
    csum = jnp.cumsum(onehot, axis=0)
    counts = csum[-1]
    rank = jnp.take_along_axis(csum, flat_e[:, None], axis=1)[:, 0] - 1
    padded = (counts + tm_e - 1) // tm_e * tm_e
    pend = jnp.cumsum(padded)
    pstart = pend - padded
    dest = (pstart[flat_e] + rank).astype(jnp.int32)
    n_slots = -(-(n_assign + n_experts * tm_e) // 1024) * 1024
    n_slots = -(-n_slots // tm_e) * tm_e
    flat_tok = jnp.arange(n_assign, dtype=jnp.int32) // TOP_K
    slot_tok = jnp.zeros((n_slots,), jnp.int32).at[dest].set(flat_tok)
    n_tiles = n_slots // tm_e
    tile_e = jnp.minimum(jnp.searchsorted(pend, jnp.arange(n_tiles, dtype=jnp.int32) * tm_e, side='right'),
                         n_experts - 1).astype(jnp.int32)
    n_used = (pend[-1] // tm_e).astype(jnp.int32).reshape(1)

    rows_per_step = 1024
    xs_sorted = _gather_rows(slot_tok, h2, rows_per_step=rows_per_step)
    ys = _moe_ffn(tile_e, n_used, xs_sorted, w1[0], b1[0], w2[0], b2[0], tm=tm_e, th=min(256, d))
    n_pad = -(-n_assign // rows_per_step) * rows_per_step
    dest_pad = jnp.pad(dest, (0, n_pad - n_assign))
    yg = _gather_rows(dest_pad, ys, rows_per_step=rows_per_step)
    yg = yg.reshape(n_pad // TOP_K, TOP_K * d)

    tm_c = min(256, t_s)
    y_p = _combine(yg, gates, x1_p, mods_p[5], pl.BlockSpec((None, 1, d), lambda i: (i // (seq // tm_c), 0, 0)),
                   tm=tm_c, row_off=0)
    y_s = _combine(yg, gates, x1_s, mods_s[5], pl.BlockSpec((tm_c, d), lambda i: (i, 0)),
                   tm=tm_c, row_off=t_p // tm_c)

    return (y_p.reshape(nb_p, seq, d), y_s.reshape(nb_s, ntok, d),
            rows_p.reshape(1, nb_p, seq, cache_w), rows_s.reshape(1, nb_s, ntok, cache_w),
            st_p[None], st_s[None])
```

```python
import functools

import jax
import jax.numpy as jnp
from jax import lax
from jax.experimental import pallas as pl
from jax.experimental.pallas import tpu as pltpu

F32 = jnp.float32
BF16 = jnp.bfloat16

NORM_EPS = 1e-6
NEG_INF = -1e30
ROPE_THETA = 10000.0
TOP_K = 4
SWIGLU_LIMIT = 7.0
SWIGLU_ALPHA = 1.702
HG_CHUNK = 64
HG_SUB = 16
LANES = 128
SUBLANES = 8
MXU_WIDTH = 256
VMEM_LIMIT_BYTES = 56 * 1024 * 1024


def _cparams(*sem):
    return pltpu.CompilerParams(dimension_semantics=sem, vmem_limit_bytes=VMEM_LIMIT_BYTES)


def _pick(n, pref, mult=LANES):
    if n <= pref:
        return n
    t = pref - pref % mult
    while t > mult and n % t:
        t -= mult
    assert n % t == 0, (n, pref, mult)
    return t


def _sigmoid(x):
    return 1.0 / (1.0 + jnp.exp(-x))


def _rms(x, eps=NORM_EPS):
    return x * lax.rsqrt(jnp.mean(x * x, axis=-1, keepdims=True) + eps)


def _dot(a, b):
    return jnp.dot(a, b, preferred_element_type=F32)


def _dot_nt(a, b):
    return lax.dot_general(a, b, (((1,), (1,)), ((), ())), preferred_element_type=F32)


def _dot_tn(a, b):
    return lax.dot_general(a, b, (((0,), (0,)), ((), ())), preferred_element_type=F32)


def _adaln_kernel(c_ref, w_ref, b_ref, o_ref):
    c = c_ref[...]
    a = (c * _sigmoid(c)).astype(BF16)
    o_ref[...] = _dot(a, w_ref[...].astype(BF16)) + b_ref[...]


def _adaln(c, w, b):
    n, d = c.shape
    nout = w.shape[1]
    tn = _pick(nout, 1024)
    return pl.pallas_call(
        _adaln_kernel,
        grid=(nout // tn,),
        in_specs=[pl.BlockSpec((n, d), lambda j: (0, 0)),
                  pl.BlockSpec((d, tn), lambda j: (0, j)),
                  pl.BlockSpec((1, tn), lambda j: (0, j))],
        out_specs=pl.BlockSpec((n, tn), lambda j: (0, j)),
        out_shape=jax.ShapeDtypeStruct((n, nout), F32),
        compiler_params=_cparams("parallel"),
        name="adaln",
    )(c, w, b.reshape(1, nout))


def _inproj_kernel(x_ref, sc_ref, sh_ref, nw_ref, wa_ref, wh_ref, qaw_ref, kvw_ref, wkr_ref, wkrs_ref,
                   cos_ref, sin_ref,
                   cqn_ref, rows_ref, latb_ref, krr_ref, krsq_ref, zh_ref, h_scr, *, q_lora, kv_lora, rope):
    j = pl.program_id(1)

    @pl.when(j == 0)
    def _():
        h = _rms(x_ref[...]) * nw_ref[...]
        h = h * (1.0 + sc_ref[...]) + sh_ref[...]
        hb = h.astype(BF16)
        h_scr[...] = hb
        za = _dot(hb, wa_ref[...])
        cqn_ref[...] = (_rms(za[:, :q_lora]) * qaw_ref[...]).astype(BF16)
        lat = _rms(za[:, q_lora:q_lora + kv_lora]) * kvw_ref[...]
        o = q_lora + kv_lora
        kr_a = za[:, o:o + LANES]
        kr_b = za[:, o + LANES:o + 2 * LANES]
        rows_ref[:, :kv_lora] = lat
        rows_ref[:, kv_lora:] = kr_a[:, :rope]
        latb_ref[...] = lat.astype(BF16)
        krr_ref[...] = kr_a * wkr_ref[...] * cos_ref[...] + kr_b * wkrs_ref[...] * sin_ref[...]
        krsq_ref[...] = jnp.broadcast_to(jnp.sum(kr_a * kr_a, axis=-1, keepdims=True), krsq_ref.shape)

    @pl.when(j > 0)
    def _():
        zh_ref[...] = _dot(h_scr[...], wh_ref[...])


def _inproj(x, sc, sh, mod_spec, nw, wa, wh, qaw, kvw, wkr, wkrs, cos_t, sin_t, tab_map, *, tm,
            q_lora, kv_lora, rope):
    t, d = x.shape
    na = wa.shape[1]
    nh = wh.shape[1]
    tn = _pick(nh, 1024)
    nj = nh // tn
    const = lambda i, j: (0, 0)
    row = lambda i, j: (i, 0)
    kern = functools.partial(_inproj_kernel, q_lora=q_lora, kv_lora=kv_lora, rope=rope)
    return pl.pallas_call(
        kern,
        grid=(t // tm, 1 + nj),
        in_specs=[pl.BlockSpec((tm, d), row), mod_spec, mod_spec,
                  pl.BlockSpec((1, d), const),
                  pl.BlockSpec((d, na), const),
                  pl.BlockSpec((d, tn), lambda i, j: (0, jnp.maximum(j - 1, 0))),
                  pl.BlockSpec((1, q_lora), const), pl.BlockSpec((1, kv_lora), const),
                  pl.BlockSpec((1, LANES), const), pl.BlockSpec((1, LANES), const),
                  pl.BlockSpec((tm, LANES), tab_map), pl.BlockSpec((tm, LANES), tab_map)],
        out_specs=[pl.BlockSpec((tm, q_lora), row),
                   pl.BlockSpec((tm, kv_lora + rope), row),
                   pl.BlockSpec((tm, kv_lora), row),
                   pl.BlockSpec((tm, LANES), row),
                   pl.BlockSpec((tm, LANES), row),
                   pl.BlockSpec((tm, tn), lambda i, j: (i, jnp.maximum(j - 1, 0)))],
        out_shape=[jax.ShapeDtypeStruct((t, q_lora), BF16),
                   jax.ShapeDtypeStruct((t, kv_lora + rope), F32),
                   jax.ShapeDtypeStruct((t, kv_lora), BF16),
                   jax.ShapeDtypeStruct((t, LANES), F32),
                   jax.ShapeDtypeStruct((t, LANES), F32),
                   jax.ShapeDtypeStruct((t, nh), F32)],
        scratch_shapes=[pltpu.VMEM((tm, d), BF16)],
        compiler_params=_cparams("parallel", "arbitrary"),
        name="inproj",
    )(x, sc, sh, nw, wa, wh, qaw, kvw, wkr, wkrs, cos_t, sin_t)


def _q_kernel(cqn_ref, w_ref, wn_ref, wa_ref, wb_ref, c_ref, s_ref, o_ref, *, scale, d_head):
    y = _dot(cqn_ref[...], w_ref[...])
    nope = y[:, :LANES]
    a = y[:, LANES:2 * LANES]
    b = y[:, 2 * LANES:]
    ssq = jnp.sum(nope * nope, axis=-1, keepdims=True) + 0.5 * jnp.sum(a * a, axis=-1, keepdims=True)
    rinv = lax.rsqrt(ssq * (1.0 / d_head) + NORM_EPS) * scale
    o_ref[:, :LANES] = (nope * rinv * wn_ref[...]).astype(o_ref.dtype)
    rot = a * wa_ref[...] * c_ref[...] + b * wb_ref[...] * s_ref[...]
    o_ref[:, LANES:] = (rot * rinv).astype(o_ref.dtype)


def _queries(cqn, wq, wn, wa, wb, tabs, tab_map, *, tm, scale, d_head):
    t, r = cqn.shape
    nheads = wq.shape[0]
    wout = 2 * LANES
    const = lambda i, h: (0, 0)
    kern = functools.partial(_q_kernel, scale=scale, d_head=d_head)
    tab_spec = pl.BlockSpec((tm, LANES), lambda i, h: tab_map(i, h))
    return pl.pallas_call(
        kern,
        grid=(t // tm, nheads),
        in_specs=[pl.BlockSpec((tm, r), lambda i, h: (i, 0)),
                  pl.BlockSpec((None, r, 3 * LANES), lambda i, h: (h, 0, 0)),
                  pl.BlockSpec((1, LANES), const), pl.BlockSpec((1, LANES), const),
                  pl.BlockSpec((1, LANES), const),
                  tab_spec, tab_spec],
        out_specs=pl.BlockSpec((None, tm, wout), lambda i, h: (h, i, 0)),
        out_shape=jax.ShapeDtypeStruct((nheads, t, wout), BF16),
        compiler_params=_cparams("parallel", "parallel"),
        name="queries",
    )(cqn, wq, wn, wa, wb, *tabs)


def _kv_kernel(latb_ref, krr_ref, krsq_ref, w_ref, wn_ref, k_ref, v_ref, *, d_head):
    y = _dot(latb_ref[...], w_ref[...])
    kn = y[:, :LANES]
    ssq = jnp.sum(kn * kn, axis=-1, keepdims=True) + krsq_ref[:, :1]
    rinv = lax.rsqrt(ssq * (1.0 / d_head) + NORM_EPS)
    k_ref[:, :LANES] = (kn * rinv * wn_ref[...]).astype(BF16)
    k_ref[:, LANES:] = (krr_ref[...] * rinv).astype(BF16)
    v_ref[...] = y[:, LANES:].astype(BF16)


def _keys_values(latb, krr, krsq, wkv, wn, *, tm, d_head):
    t, c = latb.shape
    nheads = wkv.shape[0]
    row = lambda i, h: (i, 0)
    return pl.pallas_call(
        functools.partial(_kv_kernel, d_head=d_head),
        grid=(t // tm, nheads),
        in_specs=[pl.BlockSpec((tm, c), row), pl.BlockSpec((tm, LANES), row), pl.BlockSpec((tm, LANES), row),
                  pl.BlockSpec((None, c, 2 * LANES), lambda i, h: (h, 0, 0)),
                  pl.BlockSpec((1, LANES), lambda i, h: (0, 0))],
        out_specs=[pl.BlockSpec((None, tm, 2 * LANES), lambda i, h: (h, i, 0)),
                   pl.BlockSpec((None, tm, LANES), lambda i, h: (h, i, 0))],
        out_shape=[jax.ShapeDtypeStruct((nheads, t, 2 * LANES), BF16),
                   jax.ShapeDtypeStruct((nheads, t, LANES), BF16)],
        compiler_params=_cparams("parallel", "parallel"),
        name="keys_values",
    )(latb, krr, krsq, wkv, wn)


def _flash_kernel(q_ref, k_ref, v_ref, o_ref, m_scr, l_scr, acc_scr, *, tq):
    qi = pl.program_id(2)
    ki = pl.program_id(3)

    @pl.when(ki == 0)
    def _():
        m_scr[...] = jnp.full(m_scr.shape, NEG_INF, F32)
        l_scr[...] = jnp.zeros(l_scr.shape, F32)
        acc_scr[...] = jnp.zeros(acc_scr.shape, F32)

    @pl.when(ki <= qi)
    def _():
        s = _dot_nt(q_ref[...], k_ref[...])
        qpos = qi * tq + lax.broadcasted_iota(jnp.int32, s.shape, 0)
        kpos = ki * tq + lax.broadcasted_iota(jnp.int32, s.shape, 1)
        s = jnp.where(kpos <= qpos, s, NEG_INF)
        m_old = m_scr[...]
        m_new = jnp.maximum(m_old, jnp.max(s, axis=-1, keepdims=True))
        alpha = jnp.exp(m_old - m_new)
        p = jnp.exp(s - m_new)
        l_scr[...] = alpha * l_scr[...] + jnp.sum(p, axis=-1, keepdims=True)
        acc_scr[...] = alpha * acc_scr[...] + _dot(p.astype(BF16), v_ref[...])
        m_scr[...] = m_new

    @pl.when(ki == pl.num_programs(3) - 1)
    def _():
        o_ref[...] = acc_scr[...] / l_scr[...]


def _flash(q, k, v, *, nbatch, seq, tq):
    nheads, t, dq = q.shape
    dv = v.shape[-1]
    nq = seq // tq
    return pl.pallas_call(
        functools.partial(_flash_kernel, tq=tq),
        grid=(nbatch, nheads, nq, nq),
        in_specs=[pl.BlockSpec((None, tq, dq), lambda b, h, qi, ki: (h, b * nq + qi, 0)),
                  pl.BlockSpec((None, tq, dq), lambda b, h, qi, ki: (h, b * nq + jnp.minimum(ki, qi), 0)),
                  pl.BlockSpec((None, tq, dv), lambda b, h, qi, ki: (h, b * nq + jnp.minimum(ki, qi), 0))],
        out_specs=pl.BlockSpec((tq, dv), lambda b, h, qi, ki: (b * nq + qi, h)),
        out_shape=jax.ShapeDtypeStruct((t, nheads * dv), F32),
        scratch_shapes=[pltpu.VMEM((tq, 1), F32), pltpu.VMEM((tq, 1), F32), pltpu.VMEM((tq, dv), F32)],
        compiler_params=_cparams("parallel", "parallel", "parallel", "arbitrary"),
        name="prompt_attention",
    )(q, k, v)


def _absorb_kernel(q_ref, wuk_ref, wkn_ref, wkr_ref, o_ref, *, kv_lora):
    q = q_ref[...].astype(F32)
    qn = (q[:, :LANES] * wkn_ref[...]).astype(BF16)
    o_ref[:, :kv_lora] = _dot_nt(qn, wuk_ref[...]).astype(BF16)
    o_ref[:, kv_lora:] = (q[:, LANES:] * wkr_ref[...]).astype(BF16)


def _absorb(q, wuk, wkn, wkr, *, kv_lora):
    nheads, t, wq = q.shape
    wout = kv_lora + LANES
    const = lambda h: (0, 0)
    return pl.pallas_call(
        functools.partial(_absorb_kernel, kv_lora=kv_lora),
        grid=(nheads,),
        in_specs=[pl.BlockSpec((None, t, wq), lambda h: (h, 0, 0)),
                  pl.BlockSpec((None, kv_lora, LANES), lambda h: (h, 0, 0)),
                  pl.BlockSpec((1, LANES), const), pl.BlockSpec((1, LANES), const)],
        out_specs=pl.BlockSpec((t, wout), lambda h: (0, h)),
        out_shape=jax.ShapeDtypeStruct((t, nheads * wout), BF16),
        compiler_params=_cparams("parallel"),
        name="absorb_queries",
    )(q, wuk, wkn, wkr)


def _sattn_kernel(pt_ref, *refs, n_group, kv_lora, rope, nheads, ntok, d_head, page):
    page_refs = refs[:n_group]
    (new_ref, qf_ref, wukt_ref, cs_ref, csn_ref, o_ref,
     lhs_scr, latb_scr, m_scr, l_scr, acc_scr) = refs[n_group:]
    step = pl.program_id(1)
    nrow = nheads * ntok
    nk_w = wukt_ref.shape[0]

    @pl.when(step == 0)
    def _():
        lhs_scr[:nk_w, :] = wukt_ref[...]
        lhs_scr[nk_w:, :] = qf_ref[:, :kv_lora]
        m_scr[...] = jnp.full(m_scr.shape, NEG_INF, F32)
        l_scr[...] = jnp.zeros(l_scr.shape, F32)
        acc_scr[...] = jnp.zeros(acc_scr.shape, F32)

    eye = (lax.broadcasted_iota(jnp.int32, (nrow, nrow), 0)
           == lax.broadcasted_iota(jnp.int32, (nrow, nrow), 1))

    def as_row(col):
        return jnp.sum(jnp.where(eye, col, 0.0), axis=0, keepdims=True)

    def scores(lat_b, kr_t, cs):
        nk = lat_b.shape[1]
        r = _dot(lhs_scr[...], lat_b)
        kn = r[:nk_w]
        kn2 = jnp.sum((kn * kn).reshape(nheads, nk_w // nheads, nk), axis=1)
        krsq = jnp.sum(kr_t * kr_t, axis=0, keepdims=True)
        rinv = lax.rsqrt((kn2 + krsq) * (1.0 / d_head) + NORM_EPS)
        rinv_rows = jnp.concatenate([rinv] * ntok, axis=0)
        f2 = (jnp.concatenate([kr_t, kr_t], axis=0) * cs).astype(BF16)
        return (r[nk_w:] + _dot(qf_ref[:, kv_lora:], f2)) * rinv_rows

    def update(s, lat_b):
        m_old = m_scr[...]
        m_new = jnp.maximum(m_old, jnp.max(s, axis=-1, keepdims=True))
        alpha = jnp.exp(m_old - m_new)
        p = jnp.exp(s - m_new)
        l_scr[...] = alpha * l_scr[...] + jnp.sum(p, axis=-1, keepdims=True)
        acc_scr[...] = acc_scr[...] * as_row(alpha) + _dot_nt(lat_b, p.astype(BF16))
        m_scr[...] = m_new

    per = max(1, min(n_group, MXU_WIDTH // page))
    s_parts = []
    for g in range(0, n_group, per):
        cols = slice(g * page, (g + per) * page)
        lat_b = jnp.concatenate([page_refs[g + u][:kv_lora, :] for u in range(per)], axis=1).astype(BF16)
        kr_t = jnp.concatenate([page_refs[g + u][kv_lora:, :] for u in range(per)], axis=1)
        latb_scr[:, cols] = lat_b
        s_parts.append(scores(lat_b, kr_t, cs_ref[:, cols]))
    update(jnp.concatenate(s_parts, axis=1), latb_scr[...])

    @pl.when(step == pl.num_programs(1) - 1)
    def _():
        key = lax.broadcasted_iota(jnp.int32, (nrow, page), 1)
        tok = lax.broadcasted_iota(jnp.int32, (nrow, page), 0) // nheads
        lat_b = new_ref[:kv_lora, :].astype(BF16)
        s_new = scores(lat_b, new_ref[kv_lora:, :], csn_ref[...])
        update(jnp.where(key <= tok, s_new, NEG_INF), lat_b)
        o_ref[...] = acc_scr[...] / as_row(l_scr[...])


def _sample_attention(page_table, cache_t, new_pages_t, qf, wukt, cs_pages, cs_new, *,
                      n_group, kv_lora, rope, nheads, ntok, d_head):
    nb, n_pages = page_table.shape
    cw, page = cache_t.shape[2], cache_t.shape[3]
    nrow = nheads * ntok
    wqf = qf.shape[-1]
    nsteps = n_pages // n_group

    def page_map(g):
        return lambda b, s, pt: (0, pt[b * n_pages + s * n_group + g], 0, 0)

    kern = functools.partial(_sattn_kernel, n_group=n_group, kv_lora=kv_lora, rope=rope, nheads=nheads,
                             ntok=ntok, d_head=d_head, page=page)
    in_specs = [pl.BlockSpec((None, None, cw, page), page_map(g)) for g in range(n_group)]
    in_specs += [pl.BlockSpec((None, cw, page), lambda b, s, pt: (b, 0, 0)),
                 pl.BlockSpec((None, nrow, wqf), lambda b, s, pt: (b, 0, 0)),
                 pl.BlockSpec(wukt.shape, lambda b, s, pt: (0, 0)),
                 pl.BlockSpec((2 * rope, n_group * page), lambda b, s, pt: (0, s)),
                 pl.BlockSpec((2 * rope, page), lambda b, s, pt: (0, 0))]
    grid_spec = pltpu.PrefetchScalarGridSpec(
        num_scalar_prefetch=1,
        grid=(nb, nsteps),
        in_specs=in_specs,
        out_specs=pl.BlockSpec((None, kv_lora, nrow), lambda b, s, pt: (b, 0, 0)),
        scratch_shapes=[pltpu.VMEM((wukt.shape[0] + nrow, kv_lora), BF16),
                        pltpu.VMEM((kv_lora, n_group * page), BF16),
                        pltpu.VMEM((nrow, 1), F32), pltpu.VMEM((nrow, 1), F32),
                        pltpu.VMEM((kv_lora, nrow), F32)])
    return pl.pallas_call(
        kern,
        grid_spec=grid_spec,
        out_shape=jax.ShapeDtypeStruct((nb, kv_lora, nrow), F32),
        compiler_params=_cparams("parallel", "arbitrary"),
        name="sample_attention",
    )(page_table.reshape(-1), *([cache_t] * n_group), new_pages_t, qf, wukt, cs_pages, cs_new)


def _uv_kernel(lm_ref, w_ref, o_ref):
    o_ref[...] = _dot(lm_ref[...].astype(BF16), w_ref[...])


def _uv_project(lm, wuv):
    t = lm.shape[0]
    nheads, c, dv = wuv.shape
    return pl.pallas_call(
        _uv_kernel,
        grid=(nheads,),
        in_specs=[pl.BlockSpec((t, c), lambda h: (0, h)),
                  pl.BlockSpec((None, c, dv), lambda h: (h, 0, 0))],
        out_specs=pl.BlockSpec((t, dv), lambda h: (0, h)),
        out_shape=jax.ShapeDtypeStruct((t, nheads * dv), F32),
        compiler_params=_cparams("parallel"),
        name="value_up_projection",
    )(lm, wuv)


def _cumsum_rows(g, chunk):
    if chunk <= SUBLANES:
        ridx = lax.broadcasted_iota(jnp.int32, g.shape, 0)
        out = jnp.zeros_like(g)
        for s in range(chunk):
            out = out + jnp.where(ridx >= s, g[s:s + 1, :], 0.0)
        return out
    rows = lax.broadcasted_iota(jnp.int32, (chunk, chunk), 0)
    cols = lax.broadcasted_iota(jnp.int32, (chunk, chunk), 1)
    tri = (cols <= rows).astype(BF16)
    hi = g.astype(BF16)
    r1 = g - hi.astype(F32)
    mid = r1.astype(BF16)
    lo = (r1 - mid.astype(F32)).astype(BF16)
    return _dot(tri, hi) + (_dot(tri, mid) + _dot(tri, lo))


def _hgrn_kernel(hq_ref, hf_ref, hi_ref, hg_ref, lbp_ref, s0_ref, gnw_ref, o_ref, sout_ref, st_scr, *,
                 chunk, sub, nchunk, t_valid, has_s0, hps):
    tb = pl.program_id(2)
    last = tb == pl.num_programs(2) - 1
    gnw = gnw_ref[...]
    for hh in range(hps):
        cs = slice(hh * LANES, (hh + 1) * LANES)

        @pl.when(tb == 0)
        def _():
            if has_s0:
                st_scr[hh] = s0_ref[hh].T
            else:
                st_scr[hh] = jnp.zeros(st_scr.shape[1:], F32)

        a = lbp_ref[:, cs]
        e = jnp.exp(a - jnp.max(a, axis=0, keepdims=True))
        lb = e[0:1, :] / jnp.sum(e, axis=0, keepdims=True)

        st = st_scr[hh]
        for c in range(nchunk):
            sl = slice(c * chunk, (c + 1) * chunk)
            zq = hq_ref[sl, cs]
            q = zq * _sigmoid(zq)
            k = (1.0 - lb) * _sigmoid(-hf_ref[sl, cs])
            g = jnp.log(1.0 - k)
            if t_valid is not None:
                valid = (tb * (nchunk * chunk) + c * chunk
                         + lax.broadcasted_iota(jnp.int32, k.shape, 0)) < t_valid
                k = jnp.where(valid, k, 0.0)
                g = jnp.where(valid, g, 0.0)
            v = hi_ref[sl, cs]
            zg = hg_ref[sl, cs]
            gate = zg * _sigmoid(zg)

            gc = _cumsum_rows(g, chunk)
            o = _dot_nt(q * jnp.exp(gc), st)
            g_last = gc[chunk - 1:chunk, :]
            u_t = _dot_tn(v, k * jnp.exp(g_last - gc))

            parts = []
            for i in range(chunk // sub):
                r0 = i * sub
                gi = gc[r0:r0 + sub, :]
                qi = q[r0:r0 + sub, :]
                if i > 0:
                    ref_row = gc[r0 - 1:r0, :]
                    a_off = _dot_nt(qi * jnp.exp(gi - ref_row), k[:r0, :] * jnp.exp(ref_row - gc[:r0, :]))
                    o_i = _dot(a_off, v[:r0, :])
                else:
                    o_i = jnp.zeros((sub, v.shape[1]), F32)
                trow = lax.broadcasted_iota(jnp.int32, (sub, 1), 0)
                for s in range(sub):
                    keep = trow >= s
                    d = jnp.where(keep, gi - gc[r0 + s:r0 + s + 1, :], 0.0)
                    w = jnp.exp(d) * qi * k[r0 + s:r0 + s + 1, :]
                    a_col = jnp.where(keep, jnp.sum(w, axis=-1, keepdims=True), 0.0)
                    o_i = o_i + a_col * v[r0 + s:r0 + s + 1, :]
                parts.append(o_i)
            o = o + (parts[0] if len(parts) == 1 else jnp.concatenate(parts, axis=0))
            st = st * jnp.exp(g_last) + u_t
            o_ref[sl, cs] = _rms(o) * gnw * gate

        st_scr[hh] = st

        @pl.when(last)
        def _():
            sout_ref[hh] = st.T


def _hgrn(zh, lbp, s0, gnw, *, nheads, hps, tblock, chunk, sub, t_valid):
    n, t, _ = zh.shape
    dk = LANES
    has_s0 = s0 is not None
    nlb = lbp.shape[0]
    ngrp = nheads // hps
    if not has_s0:
        s0 = jnp.zeros((1, hps, dk, dk), F32)
        s0_spec = pl.BlockSpec((None, hps, dk, dk), lambda b, h, tb: (0, 0, 0, 0))
    else:
        s0_spec = pl.BlockSpec((None, hps, dk, dk), lambda b, h, tb: (b, h, 0, 0))

    def col(group):
        return pl.BlockSpec((None, tblock, hps * dk), lambda b, h, tb: (b, tb, group * ngrp + h))

    kern = functools.partial(_hgrn_kernel, chunk=chunk, sub=sub, nchunk=tblock // chunk, t_valid=t_valid,
                             has_s0=has_s0, hps=hps)
    return pl.pallas_call(
        kern,
        grid=(n, ngrp, t // tblock),
        in_specs=[col(0), col(1), col(2), col(3),
                  pl.BlockSpec((nlb, hps * dk), lambda b, h, tb: (0, h)),
                  s0_spec,
                  pl.BlockSpec((1, dk), lambda b, h, tb: (0, 0))],
        out_specs=[pl.BlockSpec((None, tblock, hps * dk), lambda b, h, tb: (b, tb, h)),
                   pl.BlockSpec((None, hps, dk, dk), lambda b, h, tb: (b, h, 0, 0))],
        out_shape=[jax.ShapeDtypeStruct((n, t, nheads * dk), F32),
                   jax.ShapeDtypeStruct((n, nheads, dk, dk), F32)],
        scratch_shapes=[pltpu.VMEM((hps, dk, dk), F32)],
        compiler_params=_cparams("parallel", "parallel", "arbitrary"),
        name="hgrn2",
    )(zh, zh, zh, zh, lbp, s0, gnw)


def _outproj_kernel(oa_ref, orec_ref, x_ref, ga_ref, scf_ref, shf_ref, gnw_ref, fnw_ref, wo_ref, rw_ref,
                    rb_ref, x1_ref, h2_ref, topi_ref, gate_ref, *, n_attn):
    oan = (_rms(oa_ref[...]) * gnw_ref[...]).astype(BF16)
    mix = _dot(oan, wo_ref[:n_attn, :]) + _dot(orec_ref[...].astype(BF16), wo_ref[n_attn:, :])
    x1 = x_ref[...] + ga_ref[...] * mix
    x1_ref[...] = x1
    h2 = _rms(x1) * fnw_ref[...] * (1.0 + scf_ref[...]) + shf_ref[...]
    h2_ref[...] = h2
    logits = jnp.dot(h2, rw_ref[...], preferred_element_type=F32,
                     precision=lax.Precision.HIGHEST) + rb_ref[...]
    lane = lax.broadcasted_iota(jnp.int32, logits.shape, 1).astype(F32)
    vals = []
    topi = jnp.zeros(logits.shape, F32)
    work = logits
    for kk in range(TOP_K):
        m = jnp.max(work, axis=-1, keepdims=True)
        idx = jnp.min(jnp.where(work == m, lane, float(LANES)), axis=-1, keepdims=True)
        vals.append(m)
        topi = jnp.where(lane == float(kk), idx, topi)
        work = jnp.where(lane == idx, -3e38, work)
    es = [jnp.exp(vv - vals[0]) for vv in vals]
    den = es[0]
    for ee in es[1:]:
        den = den + ee
    gates = jnp.zeros(logits.shape, F32)
    for kk in range(TOP_K):
        gates = jnp.where(lane == float(kk), es[kk] / den, gates)
    topi_ref[...] = topi.astype(jnp.int32)
    gate_ref[...] = gates


def _outproj(oa, orec, x, ga, scf, shf, mod_spec, gnw, fnw, wo, rw, rb, *, tm):
    t, d = x.shape
    n_attn = oa.shape[1]
    n_rec = orec.shape[1]
    row = lambda i: (i, 0)
    const = lambda i: (0, 0)
    return pl.pallas_call(
        functools.partial(_outproj_kernel, n_attn=n_attn),
        grid=(t // tm,),
        in_specs=[pl.BlockSpec((tm, n_attn), row), pl.BlockSpec((tm, n_rec), row), pl.BlockSpec((tm, d), row),
                  mod_spec, mod_spec, mod_spec,
                  pl.BlockSpec((1, n_attn), const), pl.BlockSpec((1, d), const),
                  pl.BlockSpec(wo.shape, const), pl.BlockSpec(rw.shape, const), pl.BlockSpec((1, LANES), const)],
        out_specs=[pl.BlockSpec((tm, d), row), pl.BlockSpec((tm, d), row),
                   pl.BlockSpec((tm, LANES), row), pl.BlockSpec((tm, LANES), row)],
        out_shape=[jax.ShapeDtypeStruct((t, d), F32), jax.ShapeDtypeStruct((t, d), F32),
                   jax.ShapeDtypeStruct((t, LANES), jnp.int32), jax.ShapeDtypeStruct((t, LANES), F32)],
        compiler_params=_cparams("parallel"),
        name="outproj_router",
    )(oa, orec, x, ga, scf, shf, gnw, fnw, wo, rw, rb)


def _gather_kernel(idx_ref, src_ref, dst_ref, sem, *, rows_per_step):
    base = pl.program_id(0) * rows_per_step

    def row_copy(r, src_row):
        return pltpu.make_async_copy(src_ref.at[src_row], dst_ref.at[base + r], sem)

    def issue(r, carry):
        row_copy(r, idx_ref[r]).start()
        return carry

    def drain(r, carry):
        row_copy(r, 0).wait()
        return carry

    lax.fori_loop(0, rows_per_step, issue, 0)
    lax.fori_loop(0, rows_per_step, drain, 0)


def _gather_rows(idx, src, *, rows_per_step):
    n = idx.shape[0]
    return pl.pallas_call(
        functools.partial(_gather_kernel, rows_per_step=rows_per_step),
        grid=(n // rows_per_step,),
        in_specs=[pl.BlockSpec((rows_per_step,), lambda i: (i,), memory_space=pltpu.SMEM),
                  pl.BlockSpec(memory_space=pl.ANY)],
        out_specs=pl.BlockSpec(memory_space=pl.ANY),
        out_shape=jax.ShapeDtypeStruct((n,) + src.shape[1:], src.dtype),
        scratch_shapes=[pltpu.SemaphoreType.DMA(())],
        compiler_params=_cparams("arbitrary"),
        name="gather_rows",
    )(idx, src)


def _moe_kernel(te_ref, nu_ref, x_ref, w1g_ref, w1l_ref, b1g_ref, b1l_ref, w2_ref, b2_ref, o_ref,
                xb_scr, acc_scr, *, nseg):
    i = pl.program_id(0)
    j = pl.program_id(1)
    tm = xb_scr.shape[0]

    @pl.when(i < nu_ref[0])
    def _():
        @pl.when(j == 0)
        def _():
            for s in range(nseg):
                xb_scr[:, s * LANES:(s + 1) * LANES] = x_ref[pl.ds(s, tm, stride=nseg), :].astype(BF16)
            acc_scr[...] = jnp.zeros(acc_scr.shape, F32)

        xb = xb_scr[...]
        yg = _dot(xb, w1g_ref[...].astype(BF16)) + b1g_ref[...]
        yl = _dot(xb, w1l_ref[...].astype(BF16)) + b1l_ref[...]
        glu = jnp.minimum(yg, SWIGLU_LIMIT)
        lin = jnp.clip(yl, -SWIGLU_LIMIT, SWIGLU_LIMIT)
        a = glu * _sigmoid(SWIGLU_ALPHA * glu) * (lin + 1.0)
        acc_scr[...] += _dot(a.astype(BF16), w2_ref[...].astype(BF16))

        @pl.when(j == pl.num_programs(1) - 1)
        def _():
            for s in range(nseg):
                seg = slice(s * LANES, (s + 1) * LANES)
                o_ref[pl.ds(s, tm, stride=nseg), :] = acc_scr[:, seg] + b2_ref[:, seg]

    @pl.when(jnp.logical_and(i >= nu_ref[0], j == 0))
    def _():
        o_ref[...] = jnp.zeros(o_ref.shape, F32)


def _moe_ffn(tile_e, n_used, xs2, w1, b1, w2, b2, *, tm, th, nseg):
    d = nseg * LANES
    p = xs2.shape[0] // nseg
    ne, _, two_de = w1.shape
    de = two_de // 2
    nj = de // th
    ntiles = p // tm

    def tile(i, nu):
        return jnp.minimum(i, nu[0] - 1)

    def hid(i, j, nu):
        return jnp.where(i < nu[0], j, nj - 1)

    grid_spec = pltpu.PrefetchScalarGridSpec(
        num_scalar_prefetch=2,
        grid=(ntiles, nj),
        in_specs=[pl.BlockSpec((tm * nseg, LANES), lambda i, j, te, nu: (tile(i, nu), 0)),
                  pl.BlockSpec((None, d, th), lambda i, j, te, nu: (te[tile(i, nu)], 0, hid(i, j, nu))),
                  pl.BlockSpec((None, d, th), lambda i, j, te, nu: (te[tile(i, nu)], 0, nj + hid(i, j, nu))),
                  pl.BlockSpec((None, 1, th), lambda i, j, te, nu: (te[tile(i, nu)], 0, hid(i, j, nu))),
                  pl.BlockSpec((None, 1, th), lambda i, j, te, nu: (te[tile(i, nu)], 0, nj + hid(i, j, nu))),
                  pl.BlockSpec((None, th, d), lambda i, j, te, nu: (te[tile(i, nu)], hid(i, j, nu), 0)),
                  pl.BlockSpec((None, 1, d), lambda i, j, te, nu: (te[tile(i, nu)], 0, 0))],
        out_specs=pl.BlockSpec((tm * nseg, LANES), lambda i, j, te, nu: (i, 0)),
        scratch_shapes=[pltpu.VMEM((tm, d), BF16), pltpu.VMEM((tm, d), F32)])
    return pl.pallas_call(
        functools.partial(_moe_kernel, nseg=nseg),
        grid_spec=grid_spec,
        out_shape=jax.ShapeDtypeStruct((p * nseg, LANES), F32),
        compiler_params=_cparams("arbitrary", "arbitrary"),
        name="moe_experts",
    )(tile_e, n_used, xs2, w1, w1, b1.reshape(ne, 1, two_de), b1.reshape(ne, 1, two_de), w2,
      b2.reshape(ne, 1, d))


def _combine_kernel(*refs, nseg):
    y_refs = refs[:TOP_K]
    gate_ref, x1_ref, gf_ref, o_ref = refs[TOP_K:]
    tm = x1_ref.shape[0]
    gates = gate_ref[...]
    gk = [gates[:, kk:kk + 1] for kk in range(TOP_K)]
    for s in range(nseg):
        seg = slice(s * LANES, (s + 1) * LANES)
        acc = gk[0] * y_refs[0][pl.ds(s, tm, stride=nseg), :]
        for kk in range(1, TOP_K):
            acc = acc + gk[kk] * y_refs[kk][pl.ds(s, tm, stride=nseg), :]
        o_ref[:, seg] = x1_ref[:, seg] + gf_ref[:, seg] * acc


def _combine(yg2, gates, x1, gf, mod_spec, *, tm, row_off, nblk_all, nseg):
    t, d = x1.shape

    def ymap(kk):
        return lambda i: (kk * nblk_all + row_off + i, 0)

    return pl.pallas_call(
        functools.partial(_combine_kernel, nseg=nseg),
        grid=(t // tm,),
        in_specs=[pl.BlockSpec((tm * nseg, LANES), ymap(kk)) for kk in range(TOP_K)] + [
            pl.BlockSpec((tm, LANES), lambda i: (i + row_off, 0)),
            pl.BlockSpec((tm, d), lambda i: (i, 0)),
            mod_spec],
        out_specs=pl.BlockSpec((tm, d), lambda i: (i, 0)),
        out_shape=jax.ShapeDtypeStruct((t, d), F32),
        compiler_params=_cparams("parallel"),
        name="moe_combine",
    )(*([yg2] * TOP_K), gates, x1, gf)


def _rope_angles(pos, half):
    inv = ROPE_THETA ** (-jnp.arange(half, dtype=F32) / half)
    ang = pos.astype(F32)[:, None] * inv[None, :]
    return jnp.cos(ang), jnp.sin(ang)


def _pad_lanes(v, width=LANES):
    v = v.reshape(1, -1).astype(F32)
    return jnp.pad(v, ((0, 0), (0, width - v.shape[1])))


def _swap_halves(v):
    h = v.shape[-1] // 2
    return jnp.concatenate([v[..., h:], v[..., :h]], -1)


def kernel(x_prompt, x_sample, cache_mla, state_hgrn, page_table, c_prompt, c_sample, w_ada, b_ada,
           attn_norm_w, w_in, q_a_norm_w, w_uq, kv_a_norm_w, w_ukv, q_head_norm_w, k_head_norm_w,
           attn_group_norm_w, hg_lower_bound, hg_norm_w, w_out, ffn_norm_w, router_w, router_b,
           w1, b1, w2, b2):
    nb_p, seq, d = x_prompt.shape
    nb_s, ntok, _ = x_sample.shape
    depth = w_ada.shape[0]
    assert depth == 1, "single-layer step"
    page = cache_mla.shape[2]
    cache_w = cache_mla.shape[3]
    n_pages = page_table.shape[1]
    past_len = n_pages * page
    q_lora, nheads, d_head = w_uq.shape[1], w_uq.shape[2], w_uq.shape[3]
    kv_lora = w_ukv.shape[1]
    rope = cache_w - kv_lora
    nope = d_head - rope
    v_head = w_ukv.shape[3] - nope
    hg_heads, hg_dk, hg_dv = state_hgrn.shape[2], state_hgrn.shape[3], state_hgrn.shape[4]
    n_experts = router_w.shape[2]
    assert nope == LANES and v_head == LANES and hg_dk == LANES and hg_dv == LANES
    assert 2 * rope == LANES and n_experts <= LANES and d % LANES == 0
    half = rope // 2
    nseg = d // LANES
    attn_scale = d_head ** -0.5
    t_p = nb_p * seq
    t_s = nb_s * ntok

    mod = _adaln(jnp.concatenate([c_prompt, c_sample], 0), w_ada[0], b_ada[0])
    mods = [mod[:, i * d:(i + 1) * d] for i in range(6)]
    mods_p = [m[:nb_p].reshape(nb_p, 1, d) for m in mods]
    mods_s = [jnp.repeat(m[nb_p:], ntok, axis=0) for m in mods]

    wi = w_in[0]
    o_kr = q_lora + kv_lora
    w_kr = wi[:, o_kr:o_kr + rope]
    zpad = jnp.zeros((d, LANES - rope), F32)
    wa = jnp.concatenate([wi[:, :o_kr], w_kr, zpad, _swap_halves(w_kr), zpad], 1).astype(BF16)
    wh = wi[:, o_kr + rope:].astype(BF16)
    wq_full = jnp.transpose(w_uq[0], (1, 0, 2))
    wq_r = wq_full[:, :, nope:]
    wq_rs = _swap_halves(wq_r)
    wq = jnp.concatenate([wq_full[:, :, :nope], wq_r, wq_rs, wq_rs, wq_r], -1).astype(BF16)
    wkv = jnp.transpose(w_ukv[0], (1, 0, 2)).astype(BF16)
    wuk = wkv[:, :, :nope]
    wuv = wkv[:, :, nope:]
    wukt = jnp.transpose(wuk, (0, 2, 1)).reshape(nheads * nope, kv_lora)
    qn_w, kn_w = q_head_norm_w[0], k_head_norm_w[0]
    qr_w, kr_w = qn_w[nope:], kn_w[nope:]
    qw_n = _pad_lanes(qn_w[:nope])
    qw_a = jnp.concatenate([qr_w, _swap_halves(qr_w)]).reshape(1, LANES)
    qw_b = jnp.concatenate([_swap_halves(qr_w), qr_w]).reshape(1, LANES)
    kw_n, kw_a, kw_b = _pad_lanes(kn_w[:nope]), _pad_lanes(kr_w), _pad_lanes(_swap_halves(kr_w))
    kw_rr = jnp.concatenate([kr_w, kr_w]).reshape(1, LANES)
    nw = attn_norm_w[0].reshape(1, d)
    qaw = q_a_norm_w[0].reshape(1, q_lora)
    kvw = kv_a_norm_w[0].reshape(1, kv_lora)

    def token_tables(pos):
        cos, sin = _rope_angles(pos, half)
        z = jnp.zeros((pos.shape[0], LANES - rope), F32)
        k_cos = jnp.concatenate([cos, cos, z], -1)
        k_sin = jnp.concatenate([-sin, sin, z], -1)
        q_cos = jnp.concatenate([cos, cos, cos, -cos], -1)
        q_sin = jnp.concatenate([-sin, sin, sin, sin], -1)
        return k_cos, k_sin, q_cos, q_sin

    tabs_p = token_tables(jnp.arange(seq))
    tabs_s = token_tables(past_len + (jnp.arange(t_s) % ntok))

    tm_p = min(512, seq)
    tm_s = t_s
    nblk = seq // tm_p
    modspec_p2 = pl.BlockSpec((None, 1, d), lambda i, j: (i // nblk, 0, 0))
    modspec_s2 = pl.BlockSpec((tm_s, d), lambda i, j: (i, 0))
    dims = dict(q_lora=q_lora, kv_lora=kv_lora, rope=rope)

    xp = x_prompt.reshape(t_p, d)
    xs = x_sample.reshape(t_s, d)
    cqn_p, rows_p, latb_p, krr_p, krsq_p, zh_p = _inproj(
        xp, mods_p[1], mods_p[0], modspec_p2, nw, wa, wh, qaw, kvw, kw_a, kw_b, tabs_p[0], tabs_p[1],
        lambda i, j: (i % nblk, 0), tm=tm_p, **dims)
    cqn_s, rows_s, latb_s, krr_s, krsq_s, zh_s = _inproj(
        xs, mods_s[1], mods_s[0], modspec_s2, nw, wa, wh, qaw, kvw, kw_a, kw_b, tabs_s[0], tabs_s[1],
        lambda i, j: (i, 0), tm=tm_s, **dims)

    q_p = _queries(cqn_p, wq, qw_n, qw_a, qw_b, tabs_p[2:], lambda i, h: (i % nblk, 0), tm=tm_p,
                   scale=attn_scale, d_head=d_head)
    k_p, v_p = _keys_values(latb_p, krr_p, krsq_p, wkv, kw_n, tm=tm_p, d_head=d_head)
    oa_p = _flash(q_p, k_p, v_p, nbatch=nb_p, seq=seq, tq=tm_p)

    q_s = _queries(cqn_s, wq, qw_n, qw_a, qw_b, tabs_s[2:], lambda i, h: (i, 0), tm=tm_s,
                   scale=attn_scale, d_head=d_head)
    qf = _absorb(q_s, wuk, kw_n, kw_rr, kv_lora=kv_lora)
    qf = qf.reshape(nb_s, ntok * nheads, kv_lora + LANES)

    def cs_table(pos):
        cos, sin = _rope_angles(pos, half)
        return jnp.concatenate([cos, cos, sin, sin], -1).T

    cs_pages = cs_table(jnp.arange(past_len))
    cs_new = cs_table(past_len + jnp.arange(page))
    cache_t = jnp.swapaxes(cache_mla, 2, 3)
    new_pages = jnp.zeros((nb_s, page, cache_w), F32).at[:, :ntok].set(rows_s.reshape(nb_s, ntok, cache_w))
    new_pages_t = jnp.swapaxes(new_pages, 1, 2)
    n_group = 8
    while n_pages % n_group:
        n_group //= 2
    lat_mix = _sample_attention(page_table, cache_t, new_pages_t, qf, wukt, cs_pages, cs_new,
                                n_group=n_group, kv_lora=kv_lora, rope=rope, nheads=nheads, ntok=ntok,
                                d_head=d_head)
    lat_mix = jnp.swapaxes(lat_mix, 1, 2)
    oa_s = _uv_project(lat_mix.reshape(t_s, nheads * kv_lora), wuv)

    gnw_h = hg_norm_w[0].reshape(1, hg_dv)
    chunk_p = min(HG_CHUNK, seq)
    orec_p, st_p = _hgrn(zh_p.reshape(nb_p, seq, -1), hg_lower_bound, None, gnw_h, nheads=hg_heads, hps=1,
                         tblock=min(256, seq), chunk=chunk_p, sub=min(HG_SUB, chunk_p), t_valid=None)
    t_pad = -(-ntok // SUBLANES) * SUBLANES
    zh_s3 = jnp.pad(zh_s.reshape(nb_s, ntok, -1), ((0, 0), (0, t_pad - ntok), (0, 0)))
    orec_s, st_s = _hgrn(zh_s3, hg_lower_bound, state_hgrn[0], gnw_h, nheads=hg_heads, hps=hg_heads,
                         tblock=t_pad, chunk=t_pad, sub=t_pad, t_valid=ntok)
    orec_s = orec_s[:, :ntok].reshape(t_s, -1)

    wo = w_out[0].astype(BF16)
    rw = jnp.pad(router_w[0], ((0, 0), (0, LANES - n_experts)))
    rb = jnp.concatenate([router_b[0], jnp.full((LANES - n_experts,), NEG_INF, F32)]).reshape(1, LANES)
    gnw_a = attn_group_norm_w[0].reshape(1, -1)
    fnw = ffn_norm_w[0].reshape(1, d)
    modspec_p1 = pl.BlockSpec((None, 1, d), lambda i: (i // nblk, 0, 0))
    modspec_s1 = pl.BlockSpec((tm_s, d), lambda i: (i, 0))
    x1_p, h2_p, ti_p, gt_p = _outproj(oa_p, orec_p.reshape(t_p, -1), xp, mods_p[2], mods_p[4], mods_p[3],
                                      modspec_p1, gnw_a, fnw, wo, rw, rb, tm=tm_p)
    x1_s, h2_s, ti_s, gt_s = _outproj(oa_s, orec_s, xs, mods_s[2], mods_s[4], mods_s[3],
                                      modspec_s1, gnw_a, fnw, wo, rw, rb, tm=tm_s)

    n_tok = t_p + t_s
    h2 = jnp.concatenate([h2_p, h2_s], 0).reshape(n_tok, nseg, LANES)
    top_i = jnp.concatenate([ti_p[:, :TOP_K], ti_s[:, :TOP_K]], 0)
    gates = jnp.concatenate([gt_p, gt_s], 0)
    tm_e = min(512, n_tok)
    rows_per_step = 1024
    n_assign = n_tok * TOP_K
    flat_e = top_i.reshape(-1)
    onehot = (flat_e[:, None] == jnp.arange(n_experts, dtype=jnp.int32)[None, :]).astype(jnp.int32)
    csum = jnp.cumsum(onehot, axis=0)
    counts = csum[-1]
    rank = jnp.take_along_axis(csum, flat_e[:, None], axis=1)[:, 0] - 1
    padded = (counts + tm_e - 1) // tm_e * tm_e
    pend = jnp.cumsum(padded)
    pstart = pend - padded
    dest = (pstart[flat_e] + rank).astype(jnp.int32)
    n_slots = -(-(n_assign + n_experts * tm_e) // rows_per_step) * rows_per_step
    n_slots = -(-n_slots // tm_e) * tm_e
    flat_tok = jnp.arange(n_assign, dtype=jnp.int32) // TOP_K
    slot_tok = jnp.zeros((n_slots,), jnp.int32).at[dest].set(flat_tok)
    n_tiles = n_slots // tm_e
    tile_e = jnp.minimum(jnp.searchsorted(pend, jnp.arange(n_tiles, dtype=jnp.int32) * tm_e, side='right'),
                         n_experts - 1).astype(jnp.int32)
    n_used = (pend[-1] // tm_e).astype(jnp.int32).reshape(1)

    xs_sorted = _gather_rows(slot_tok, h2, rows_per_step=rows_per_step)
    ys2 = _moe_ffn(tile_e, n_used, xs_sorted.reshape(n_slots * nseg, LANES), w1[0], b1[0], w2[0], b2[0],
                   tm=tm_e, th=min(256, d), nseg=nseg)
    tm_c = min(256, t_s)
    assert n_tok % tm_c == 0 and t_p % tm_c == 0
    dest_kmajor = dest.reshape(n_tok, TOP_K).T.reshape(-1)
    n_pad = -(-n_assign // rows_per_step) * rows_per_step
    dest_kmajor = jnp.pad(dest_kmajor, (0, n_pad - n_assign))
    yg = _gather_rows(dest_kmajor, ys2.reshape(n_slots, nseg, LANES), rows_per_step=rows_per_step)
    yg2 = yg.reshape(n_pad * nseg, LANES)

    nblk_all = n_tok // tm_c
    y_p = _combine(yg2, gates, x1_p, mods_p[5], pl.BlockSpec((None, 1, d), lambda i: (i // (seq // tm_c), 0, 0)),
                   tm=tm_c, row_off=0, nblk_all=nblk_all, nseg=nseg)
    y_s = _combine(yg2, gates, x1_s, mods_s[5], pl.BlockSpec((tm_c, d), lambda i: (i, 0)),
                   tm=tm_c, row_off=t_p // tm_c, nblk_all=nblk_all, nseg=nseg)

    return (y_p.reshape(nb_p, seq, d), y_s.reshape(nb_s, ntok, d),
            rows_p.reshape(1, nb_p, seq, cache_w), rows_s.reshape(1, nb_s, ntok, cache_w),
            st_p[None], st_s[None])
```

```python
import functools

import jax
import jax.numpy as jnp
from jax import lax
from jax.experimental import pallas as pl
from jax.experimental.pallas import tpu as pltpu

F32 = jnp.float32
BF16 = jnp.bfloat16

NORM_EPS = 1e-6
NEG_INF = -1e30
ROPE_THETA = 10000.0
TOP_K = 4
SWIGLU_LIMIT = 7.0
SWIGLU_ALPHA = 1.702
HG_CHUNK = 64
HG_SUB = 16
LANES = 128
SUBLANES = 8
MXU_WIDTH = 256
VMEM_LIMIT_BYTES = 56 * 1024 * 1024


def _cparams(*sem):
    return pltpu.CompilerParams(dimension_semantics=sem, vmem_limit_bytes=VMEM_LIMIT_BYTES)


def _pick(n, pref, mult=LANES):
    if n <= pref:
        return n
    t = pref - pref % mult
    while t > mult and n % t:
        t -= mult
    assert n % t == 0, (n, pref, mult)
    return t


def _sigmoid(x):
    return 1.0 / (1.0 + jnp.exp(-x))


def _rms(x, eps=NORM_EPS):
    return x * lax.rsqrt(jnp.mean(x * x, axis=-1, keepdims=True) + eps)


def _dot(a, b):
    return jnp.dot(a, b, preferred_element_type=F32)


def _dot_nt(a, b):
    return lax.dot_general(a, b, (((1,), (1,)), ((), ())), preferred_element_type=F32)


def _dot_tn(a, b):
    return lax.dot_general(a, b, (((0,), (0,)), ((), ())), preferred_element_type=F32)


def _adaln_kernel(c_ref, w_ref, b_ref, o_ref):
    c = c_ref[...]
    a = (c * _sigmoid(c)).astype(BF16)
    o_ref[...] = _dot(a, w_ref[...].astype(BF16)) + b_ref[...]


def _adaln(c, w, b):
    n, d = c.shape
    nout = w.shape[1]
    tn = _pick(nout, 1024)
    return pl.pallas_call(
        _adaln_kernel,
        grid=(nout // tn,),
        in_specs=[pl.BlockSpec((n, d), lambda j: (0, 0)),
                  pl.BlockSpec((d, tn), lambda j: (0, j)),
                  pl.BlockSpec((1, tn), lambda j: (0, j))],
        out_specs=pl.BlockSpec((n, tn), lambda j: (0, j)),
        out_shape=jax.ShapeDtypeStruct((n, nout), F32),
        compiler_params=_cparams("parallel"),
        name="adaln",
    )(c, w, b.reshape(1, nout))


def _inproj_kernel(x_ref, sc_ref, sh_ref, nw_ref, wa_ref, wh_ref, qaw_ref, kvw_ref, wkr_ref, wkrs_ref,
                   cos_ref, sin_ref,
                   cqn_ref, rows_ref, latb_ref, krr_ref, krsq_ref, zh_ref, h_scr, *, q_lora, kv_lora, rope):
    j = pl.program_id(1)

    @pl.when(j == 0)
    def _():
        h = _rms(x_ref[...]) * nw_ref[...]
        h = h * (1.0 + sc_ref[...]) + sh_ref[...]
        hb = h.astype(BF16)
        h_scr[...] = hb
        za = _dot(hb, wa_ref[...])
        cqn_ref[...] = (_rms(za[:, :q_lora]) * qaw_ref[...]).astype(BF16)
        lat = _rms(za[:, q_lora:q_lora + kv_lora]) * kvw_ref[...]
        o = q_lora + kv_lora
        kr_a = za[:, o:o + LANES]
        kr_b = za[:, o + LANES:o + 2 * LANES]
        rows_ref[:, :kv_lora] = lat
        rows_ref[:, kv_lora:] = kr_a[:, :rope]
        latb_ref[...] = lat.astype(BF16)
        krr_ref[...] = kr_a * wkr_ref[...] * cos_ref[...] + kr_b * wkrs_ref[...] * sin_ref[...]
        krsq_ref[...] = jnp.broadcast_to(jnp.sum(kr_a * kr_a, axis=-1, keepdims=True), krsq_ref.shape)

    @pl.when(j > 0)
    def _():
        zh_ref[...] = _dot(h_scr[...], wh_ref[...])


def _inproj(x, sc, sh, mod_spec, nw, wa, wh, qaw, kvw, wkr, wkrs, cos_t, sin_t, tab_map, *, tm,
            q_lora, kv_lora, rope):
    t, d = x.shape
    na = wa.shape[1]
    nh = wh.shape[1]
    tn = _pick(nh, 1024)
    nj = nh // tn
    const = lambda i, j: (0, 0)
    row = lambda i, j: (i, 0)
    kern = functools.partial(_inproj_kernel, q_lora=q_lora, kv_lora=kv_lora, rope=rope)
    return pl.pallas_call(
        kern,
        grid=(t // tm, 1 + nj),
        in_specs=[pl.BlockSpec((tm, d), row), mod_spec, mod_spec,
                  pl.BlockSpec((1, d), const),
                  pl.BlockSpec((d, na), const),
                  pl.BlockSpec((d, tn), lambda i, j: (0, jnp.maximum(j - 1, 0))),
                  pl.BlockSpec((1, q_lora), const), pl.BlockSpec((1, kv_lora), const),
                  pl.BlockSpec((1, LANES), const), pl.BlockSpec((1, LANES), const),
                  pl.BlockSpec((tm, LANES), tab_map), pl.BlockSpec((tm, LANES), tab_map)],
        out_specs=[pl.BlockSpec((tm, q_lora), row),
                   pl.BlockSpec((tm, kv_lora + rope), row),
                   pl.BlockSpec((tm, kv_lora), row),
                   pl.BlockSpec((tm, LANES), row),
                   pl.BlockSpec((tm, LANES), row),
                   pl.BlockSpec((tm, tn), lambda i, j: (i, jnp.maximum(j - 1, 0)))],
        out_shape=[jax.ShapeDtypeStruct((t, q_lora), BF16),
                   jax.ShapeDtypeStruct((t, kv_lora + rope), F32),
                   jax.ShapeDtypeStruct((t, kv_lora), BF16),
                   jax.ShapeDtypeStruct((t, LANES), F32),
                   jax.ShapeDtypeStruct((t, LANES), F32),
                   jax.ShapeDtypeStruct((t, nh), F32)],
        scratch_shapes=[pltpu.VMEM((tm, d), BF16)],
        compiler_params=_cparams("parallel", "arbitrary"),
        name="inproj",
    )(x, sc, sh, nw, wa, wh, qaw, kvw, wkr, wkrs, cos_t, sin_t)


def _q_kernel(cqn_ref, w_ref, wn_ref, wa_ref, wb_ref, c_ref, s_ref, o_ref, *, scale, d_head):
    y = _dot(cqn_ref[...], w_ref[...])
    nope = y[:, :LANES]
    a = y[:, LANES:2 * LANES]
    b = y[:, 2 * LANES:]
    ssq = jnp.sum(nope * nope, axis=-1, keepdims=True) + 0.5 * jnp.sum(a * a, axis=-1, keepdims=True)
    rinv = lax.rsqrt(ssq * (1.0 / d_head) + NORM_EPS) * scale
    o_ref[:, :LANES] = (nope * rinv * wn_ref[...]).astype(o_ref.dtype)
    rot = a * wa_ref[...] * c_ref[...] + b * wb_ref[...] * s_ref[...]
    o_ref[:, LANES:] = (rot * rinv).astype(o_ref.dtype)


def _queries(cqn, wq, wn, wa, wb, tabs, tab_map, *, tm, scale, d_head):
    t, r = cqn.shape
    nheads = wq.shape[0]
    wout = 2 * LANES
    const = lambda i, h: (0, 0)
    kern = functools.partial(_q_kernel, scale=scale, d_head=d_head)
    tab_spec = pl.BlockSpec((tm, LANES), lambda i, h: tab_map(i, h))
    return pl.pallas_call(
        kern,
        grid=(t // tm, nheads),
        in_specs=[pl.BlockSpec((tm, r), lambda i, h: (i, 0)),
                  pl.BlockSpec((None, r, 3 * LANES), lambda i, h: (h, 0, 0)),
                  pl.BlockSpec((1, LANES), const), pl.BlockSpec((1, LANES), const),
                  pl.BlockSpec((1, LANES), const),
                  tab_spec, tab_spec],
        out_specs=pl.BlockSpec((None, tm, wout), lambda i, h: (h, i, 0)),
        out_shape=jax.ShapeDtypeStruct((nheads, t, wout), BF16),
        compiler_params=_cparams("parallel", "parallel"),
        name="queries",
    )(cqn, wq, wn, wa, wb, *tabs)


def _kv_kernel(latb_ref, krr_ref, krsq_ref, w_ref, wn_ref, k_ref, v_ref, *, d_head):
    y = _dot(latb_ref[...], w_ref[...])
    kn = y[:, :LANES]
    ssq = jnp.sum(kn * kn, axis=-1, keepdims=True) + krsq_ref[:, :1]
    rinv = lax.rsqrt(ssq * (1.0 / d_head) + NORM_EPS)
    k_ref[:, :LANES] = (kn * rinv * wn_ref[...]).astype(BF16)
    k_ref[:, LANES:] = (krr_ref[...] * rinv).astype(BF16)
    v_ref[...] = y[:, LANES:].astype(BF16)


def _keys_values(latb, krr, krsq, wkv, wn, *, tm, d_head):
    t, c = latb.shape
    nheads = wkv.shape[0]
    row = lambda i, h: (i, 0)
    return pl.pallas_call(
        functools.partial(_kv_kernel, d_head=d_head),
        grid=(t // tm, nheads),
        in_specs=[pl.BlockSpec((tm, c), row), pl.BlockSpec((tm, LANES), row), pl.BlockSpec((tm, LANES), row),
                  pl.BlockSpec((None, c, 2 * LANES), lambda i, h: (h, 0, 0)),
                  pl.BlockSpec((1, LANES), lambda i, h: (0, 0))],
        out_specs=[pl.BlockSpec((None, tm, 2 * LANES), lambda i, h: (h, i, 0)),
                   pl.BlockSpec((None, tm, LANES), lambda i, h: (h, i, 0))],
        out_shape=[jax.ShapeDtypeStruct((nheads, t, 2 * LANES), BF16),
                   jax.ShapeDtypeStruct((nheads, t, LANES), BF16)],
        compiler_params=_cparams("parallel", "parallel"),
        name="keys_values",
    )(latb, krr, krsq, wkv, wn)


def _flash_kernel(q_ref, k_ref, v_ref, o_ref, m_scr, l_scr, acc_scr, *, tq):
    qi = pl.program_id(2)
    ki = pl.program_id(3)

    @pl.when(ki == 0)
    def _():
        m_scr[...] = jnp.full(m_scr.shape, NEG_INF, F32)
        l_scr[...] = jnp.zeros(l_scr.shape, F32)
        acc_scr[...] = jnp.zeros(acc_scr.shape, F32)

    @pl.when(ki <= qi)
    def _():
        s = _dot_nt(q_ref[...], k_ref[...])
        qpos = qi * tq + lax.broadcasted_iota(jnp.int32, s.shape, 0)
        kpos = ki * tq + lax.broadcasted_iota(jnp.int32, s.shape, 1)
        s = jnp.where(kpos <= qpos, s, NEG_INF)
        m_old = m_scr[...]
        m_new = jnp.maximum(m_old, jnp.max(s, axis=-1, keepdims=True))
        alpha = jnp.exp(m_old - m_new)
        p = jnp.exp(s - m_new)
        l_scr[...] = alpha * l_scr[...] + jnp.sum(p, axis=-1, keepdims=True)
        acc_scr[...] = alpha * acc_scr[...] + _dot(p.astype(BF16), v_ref[...])
        m_scr[...] = m_new

    @pl.when(ki == pl.num_programs(3) - 1)
    def _():
        o_ref[...] = acc_scr[...] / l_scr[...]


def _flash(q, k, v, *, nbatch, seq, tq):
    nheads, t, dq = q.shape
    dv = v.shape[-1]
    nq = seq // tq
    return pl.pallas_call(
        functools.partial(_flash_kernel, tq=tq),
        grid=(nbatch, nheads, nq, nq),
        in_specs=[pl.BlockSpec((None, tq, dq), lambda b, h, qi, ki: (h, b * nq + qi, 0)),
                  pl.BlockSpec((None, tq, dq), lambda b, h, qi, ki: (h, b * nq + jnp.minimum(ki, qi), 0)),
                  pl.BlockSpec((None, tq, dv), lambda b, h, qi, ki: (h, b * nq + jnp.minimum(ki, qi), 0))],
        out_specs=pl.BlockSpec((tq, dv), lambda b, h, qi, ki: (b * nq + qi, h)),
        out_shape=jax.ShapeDtypeStruct((t, nheads * dv), F32),
        scratch_shapes=[pltpu.VMEM((tq, 1), F32), pltpu.VMEM((tq, 1), F32), pltpu.VMEM((tq, dv), F32)],
        compiler_params=_cparams("parallel", "parallel", "parallel", "arbitrary"),
        name="prompt_attention",
    )(q, k, v)


def _absorb_kernel(q_ref, wuk_ref, wkn_ref, wkr_ref, o_ref, *, kv_lora):
    q = q_ref[...].astype(F32)
    qn = (q[:, :LANES] * wkn_ref[...]).astype(BF16)
    o_ref[:, :kv_lora] = _dot_nt(qn, wuk_ref[...]).astype(BF16)
    o_ref[:, kv_lora:] = (q[:, LANES:] * wkr_ref[...]).astype(BF16)


def _absorb(q, wuk, wkn, wkr, *, kv_lora):
    nheads, t, wq = q.shape
    wout = kv_lora + LANES
    const = lambda h: (0, 0)
    return pl.pallas_call(
        functools.partial(_absorb_kernel, kv_lora=kv_lora),
        grid=(nheads,),
        in_specs=[pl.BlockSpec((None, t, wq), lambda h: (h, 0, 0)),
                  pl.BlockSpec((None, kv_lora, LANES), lambda h: (h, 0, 0)),
                  pl.BlockSpec((1, LANES), const), pl.BlockSpec((1, LANES), const)],
        out_specs=pl.BlockSpec((t, wout), lambda h: (0, h)),
        out_shape=jax.ShapeDtypeStruct((t, nheads * wout), BF16),
        compiler_params=_cparams("parallel"),
        name="absorb_queries",
    )(q, wuk, wkn, wkr)


def _sattn_kernel(pt_ref, *refs, n_group, kv_lora, rope, nheads, ntok, d_head, page):
    page_refs = refs[:n_group]
    (new_ref, qf_ref, wukt_ref, cs_ref, csn_ref, o_ref,
     lhs_scr, latb_scr, m_scr, l_scr, acc_scr) = refs[n_group:]
    step = pl.program_id(1)
    nrow = nheads * ntok
    nk_w = wukt_ref.shape[0]

    @pl.when(step == 0)
    def _():
        lhs_scr[:nk_w, :] = wukt_ref[...]
        lhs_scr[nk_w:, :] = qf_ref[:, :kv_lora]
        m_scr[...] = jnp.full(m_scr.shape, NEG_INF, F32)
        l_scr[...] = jnp.zeros(l_scr.shape, F32)
        acc_scr[...] = jnp.zeros(acc_scr.shape, F32)

    eye = (lax.broadcasted_iota(jnp.int32, (nrow, nrow), 0)
           == lax.broadcasted_iota(jnp.int32, (nrow, nrow), 1))

    def as_row(col):
        return jnp.sum(jnp.where(eye, col, 0.0), axis=0, keepdims=True)

    def scores(lat_b, kr_t, cs):
        nk = lat_b.shape[1]
        r = _dot(lhs_scr[...], lat_b)
        kn = r[:nk_w]
        kn2 = jnp.sum((kn * kn).reshape(nheads, nk_w // nheads, nk), axis=1)
        krsq = jnp.sum(kr_t * kr_t, axis=0, keepdims=True)
        rinv = lax.rsqrt((kn2 + krsq) * (1.0 / d_head) + NORM_EPS)
        rinv_rows = jnp.concatenate([rinv] * ntok, axis=0)
        f2 = (jnp.concatenate([kr_t, kr_t], axis=0) * cs).astype(BF16)
        return (r[nk_w:] + _dot(qf_ref[:, kv_lora:], f2)) * rinv_rows

    def update(s, lat_b):
        m_old = m_scr[...]
        m_new = jnp.maximum(m_old, jnp.max(s, axis=-1, keepdims=True))
        alpha = jnp.exp(m_old - m_new)
        p = jnp.exp(s - m_new)
        l_scr[...] = alpha * l_scr[...] + jnp.sum(p, axis=-1, keepdims=True)
        acc_scr[...] = acc_scr[...] * as_row(alpha) + _dot_nt(lat_b, p.astype(BF16))
        m_scr[...] = m_new

    per = max(1, min(n_group, MXU_WIDTH // page))
    s_parts = []
    for g in range(0, n_group, per):
        cols = slice(g * page, (g + per) * page)
        lat_b = jnp.concatenate([page_refs[g + u][:kv_lora, :] for u in range(per)], axis=1).astype(BF16)
        kr_t = jnp.concatenate([page_refs[g + u][kv_lora:, :] for u in range(per)], axis=1)
        latb_scr[:, cols] = lat_b
        s_parts.append(scores(lat_b, kr_t, cs_ref[:, cols]))
    update(jnp.concatenate(s_parts, axis=1), latb_scr[...])

    @pl.when(step == pl.num_programs(1) - 1)
    def _():
        key = lax.broadcasted_iota(jnp.int32, (nrow, page), 1)
        tok = lax.broadcasted_iota(jnp.int32, (nrow, page), 0) // nheads
        lat_b = new_ref[:kv_lora, :].astype(BF16)
        s_new = scores(lat_b, new_ref[kv_lora:, :], csn_ref[...])
        update(jnp.where(key <= tok, s_new, NEG_INF), lat_b)
        o_ref[...] = acc_scr[...] / as_row(l_scr[...])


def _sample_attention(page_table, cache_t, new_pages_t, qf, wukt, cs_pages, cs_new, *,
                      n_group, kv_lora, rope, nheads, ntok, d_head):
    nb, n_pages = page_table.shape
    cw, page = cache_t.shape[2], cache_t.shape[3]
    nrow = nheads * ntok
    wqf = qf.shape[-1]
    nsteps = n_pages // n_group

    def page_map(g):
        return lambda b, s, pt: (0, pt[b * n_pages + s * n_group + g], 0, 0)

    kern = functools.partial(_sattn_kernel, n_group=n_group, kv_lora=kv_lora, rope=rope, nheads=nheads,
                             ntok=ntok, d_head=d_head, page=page)
    in_specs = [pl.BlockSpec((None, None, cw, page), page_map(g)) for g in range(n_group)]
    in_specs += [pl.BlockSpec((None, cw, page), lambda b, s, pt: (b, 0, 0)),
                 pl.BlockSpec((None, nrow, wqf), lambda b, s, pt: (b, 0, 0)),
                 pl.BlockSpec(wukt.shape, lambda b, s, pt: (0, 0)),
                 pl.BlockSpec((2 * rope, n_group * page), lambda b, s, pt: (0, s)),
                 pl.BlockSpec((2 * rope, page), lambda b, s, pt: (0, 0))]
    grid_spec = pltpu.PrefetchScalarGridSpec(
        num_scalar_prefetch=1,
        grid=(nb, nsteps),
        in_specs=in_specs,
        out_specs=pl.BlockSpec((None, kv_lora, nrow), lambda b, s, pt: (b, 0, 0)),
        scratch_shapes=[pltpu.VMEM((wukt.shape[0] + nrow, kv_lora), BF16),
                        pltpu.VMEM((kv_lora, n_group * page), BF16),
                        pltpu.VMEM((nrow, 1), F32), pltpu.VMEM((nrow, 1), F32),
                        pltpu.VMEM((kv_lora, nrow), F32)])
    return pl.pallas_call(
        kern,
        grid_spec=grid_spec,
        out_shape=jax.ShapeDtypeStruct((nb, kv_lora, nrow), F32),
        compiler_params=_cparams("parallel", "arbitrary"),
        name="sample_attention",
    )(page_table.reshape(-1), *([cache_t] * n_group), new_pages_t, qf, wukt, cs_pages, cs_new)


def _uv_kernel(lm_ref, w_ref, o_ref):
    o_ref[...] = _dot(lm_ref[...].astype(BF16), w_ref[...])


def _uv_project(lm, wuv):
    t = lm.shape[0]
    nheads, c, dv = wuv.shape
    return pl.pallas_call(
        _uv_kernel,
        grid=(nheads,),
        in_specs=[pl.BlockSpec((t, c), lambda h: (0, h)),
                  pl.BlockSpec((None, c, dv), lambda h: (h, 0, 0))],
        out_specs=pl.BlockSpec((t, dv), lambda h: (0, h)),
        out_shape=jax.ShapeDtypeStruct((t, nheads * dv), F32),
        compiler_params=_cparams("parallel"),
        name="value_up_projection",
    )(lm, wuv)


def _cumsum_rows(g, chunk):
    if chunk <= SUBLANES:
        ridx = lax.broadcasted_iota(jnp.int32, g.shape, 0)
        out = jnp.zeros_like(g)
        for s in range(chunk):
            out = out + jnp.where(ridx >= s, g[s:s + 1, :], 0.0)
        return out
    rows = lax.broadcasted_iota(jnp.int32, (chunk, chunk), 0)
    cols = lax.broadcasted_iota(jnp.int32, (chunk, chunk), 1)
    tri = (cols <= rows).astype(BF16)
    hi = g.astype(BF16)
    r1 = g - hi.astype(F32)
    mid = r1.astype(BF16)
    lo = (r1 - mid.astype(F32)).astype(BF16)
    return _dot(tri, hi) + (_dot(tri, mid) + _dot(tri, lo))


def _hgrn_kernel(hq_ref, hf_ref, hi_ref, hg_ref, lbp_ref, s0_ref, gnw_ref, o_ref, sout_ref, st_scr, *,
                 chunk, sub, nchunk, t_valid, has_s0, hps):
    tb = pl.program_id(2)
    last = tb == pl.num_programs(2) - 1
    gnw = gnw_ref[...]
    for hh in range(hps):
        cs = slice(hh * LANES, (hh + 1) * LANES)

        @pl.when(tb == 0)
        def _():
            if has_s0:
                st_scr[hh] = s0_ref[hh].T
            else:
                st_scr[hh] = jnp.zeros(st_scr.shape[1:], F32)

        a = lbp_ref[:, cs]
        e = jnp.exp(a - jnp.max(a, axis=0, keepdims=True))
        lb = e[0:1, :] / jnp.sum(e, axis=0, keepdims=True)

        st = st_scr[hh]
        for c in range(nchunk):
            sl = slice(c * chunk, (c + 1) * chunk)
            zq = hq_ref[sl, cs]
            q = zq * _sigmoid(zq)
            k = (1.0 - lb) * _sigmoid(-hf_ref[sl, cs])
            g = jnp.log(1.0 - k)
            if t_valid is not None:
                valid = (tb * (nchunk * chunk) + c * chunk
                         + lax.broadcasted_iota(jnp.int32, k.shape, 0)) < t_valid
                k = jnp.where(valid, k, 0.0)
                g = jnp.where(valid, g, 0.0)
            v = hi_ref[sl, cs]
            zg = hg_ref[sl, cs]
            gate = zg * _sigmoid(zg)

            gc = _cumsum_rows(g, chunk)
            o = _dot_nt(q * jnp.exp(gc), st)
            g_last = gc[chunk - 1:chunk, :]
            u_t = _dot_tn(v, k * jnp.exp(g_last - gc))

            parts = []
            for i in range(chunk // sub):
                r0 = i * sub
                gi = gc[r0:r0 + sub, :]
                qi = q[r0:r0 + sub, :]
                if i > 0:
                    ref_row = gc[r0 - 1:r0, :]
                    a_off = _dot_nt(qi * jnp.exp(gi - ref_row), k[:r0, :] * jnp.exp(ref_row - gc[:r0, :]))
                    o_i = _dot(a_off, v[:r0, :])
                else:
                    o_i = jnp.zeros((sub, v.shape[1]), F32)
                trow = lax.broadcasted_iota(jnp.int32, (sub, 1), 0)
                for s in range(sub):
                    keep = trow >= s
                    d = jnp.where(keep, gi - gc[r0 + s:r0 + s + 1, :], 0.0)
                    w = jnp.exp(d) * qi * k[r0 + s:r0 + s + 1, :]
                    a_col = jnp.where(keep, jnp.sum(w, axis=-1, keepdims=True), 0.0)
                    o_i = o_i + a_col * v[r0 + s:r0 + s + 1, :]
                parts.append(o_i)
            o = o + (parts[0] if len(parts) == 1 else jnp.concatenate(parts, axis=0))
            st = st * jnp.exp(g_last) + u_t
            o_ref[sl, cs] = _rms(o) * gnw * gate

        st_scr[hh] = st

        @pl.when(last)
        def _():
            sout_ref[hh] = st.T


def _hgrn(zh, lbp, s0, gnw, *, nheads, hps, tblock, chunk, sub, t_valid):
    n, t, _ = zh.shape
    dk = LANES
    has_s0 = s0 is not None
    nlb = lbp.shape[0]
    ngrp = nheads // hps
    if not has_s0:
        s0 = jnp.zeros((1, hps, dk, dk), F32)
        s0_spec = pl.BlockSpec((None, hps, dk, dk), lambda b, h, tb: (0, 0, 0, 0))
    else:
        s0_spec = pl.BlockSpec((None, hps, dk, dk), lambda b, h, tb: (b, h, 0, 0))

    def col(group):
        return pl.BlockSpec((None, tblock, hps * dk), lambda b, h, tb: (b, tb, group * ngrp + h))

    kern = functools.partial(_hgrn_kernel, chunk=chunk, sub=sub, nchunk=tblock // chunk, t_valid=t_valid,
                             has_s0=has_s0, hps=hps)
    return pl.pallas_call(
        kern,
        grid=(n, ngrp, t // tblock),
        in_specs=[col(0), col(1), col(2), col(3),
                  pl.BlockSpec((nlb, hps * dk), lambda b, h, tb: (0, h)),
                  s0_spec,
                  pl.BlockSpec((1, dk), lambda b, h, tb: (0, 0))],
        out_specs=[pl.BlockSpec((None, tblock, hps * dk), lambda b, h, tb: (b, tb, h)),
                   pl.BlockSpec((None, hps, dk, dk), lambda b, h, tb: (b, h, 0, 0))],
        out_shape=[jax.ShapeDtypeStruct((n, t, nheads * dk), F32),
                   jax.ShapeDtypeStruct((n, nheads, dk, dk), F32)],
        scratch_shapes=[pltpu.VMEM((hps, dk, dk), F32)],
        compiler_params=_cparams("parallel", "parallel", "arbitrary"),
        name="hgrn2",
    )(zh, zh, zh, zh, lbp, s0, gnw)


def _outproj_kernel(oa_ref, orec_ref, x_ref, ga_ref, scf_ref, shf_ref, gnw_ref, fnw_ref, wo_ref, rw_ref,
                    rb_ref, x1_ref, h2_ref, topi_ref, gate_ref, *, n_attn):
    oan = (_rms(oa_ref[...]) * gnw_ref[...]).astype(BF16)
    mix = _dot(oan, wo_ref[:n_attn, :]) + _dot(orec_ref[...].astype(BF16), wo_ref[n_attn:, :])
    x1 = x_ref[...] + ga_ref[...] * mix
    x1_ref[...] = x1
    h2 = _rms(x1) * fnw_ref[...] * (1.0 + scf_ref[...]) + shf_ref[...]
    h2_ref[...] = h2
    logits = jnp.dot(h2, rw_ref[...], preferred_element_type=F32,
                     precision=lax.Precision.HIGHEST) + rb_ref[...]
    lane = lax.broadcasted_iota(jnp.int32, logits.shape, 1).astype(F32)
    vals = []
    topi = jnp.zeros(logits.shape, F32)
    work = logits
    for kk in range(TOP_K):
        m = jnp.max(work, axis=-1, keepdims=True)
        idx = jnp.min(jnp.where(work == m, lane, float(LANES)), axis=-1, keepdims=True)
        vals.append(m)
        topi = jnp.where(lane == float(kk), idx, topi)
        work = jnp.where(lane == idx, -3e38, work)
    es = [jnp.exp(vv - vals[0]) for vv in vals]
    den = es[0]
    for ee in es[1:]:
        den = den + ee
    gates = jnp.zeros(logits.shape, F32)
    for kk in range(TOP_K):
        gates = jnp.where(lane == float(kk), es[kk] / den, gates)
    topi_ref[...] = topi.astype(jnp.int32)
    gate_ref[...] = gates


def _outproj(oa, orec, x, ga, scf, shf, mod_spec, gnw, fnw, wo, rw, rb, *, tm):
    t, d = x.shape
    n_attn = oa.shape[1]
    n_rec = orec.shape[1]
    row = lambda i: (i, 0)
    const = lambda i: (0, 0)
    return pl.pallas_call(
        functools.partial(_outproj_kernel, n_attn=n_attn),
        grid=(t // tm,),
        in_specs=[pl.BlockSpec((tm, n_attn), row), pl.BlockSpec((tm, n_rec), row), pl.BlockSpec((tm, d), row),
                  mod_spec, mod_spec, mod_spec,
                  pl.BlockSpec((1, n_attn), const), pl.BlockSpec((1, d), const),
                  pl.BlockSpec(wo.shape, const), pl.BlockSpec(rw.shape, const), pl.BlockSpec((1, LANES), const)],
        out_specs=[pl.BlockSpec((tm, d), row), pl.BlockSpec((tm, d), row),
                   pl.BlockSpec((tm, LANES), row), pl.BlockSpec((tm, LANES), row)],
        out_shape=[jax.ShapeDtypeStruct((t, d), F32), jax.ShapeDtypeStruct((t, d), F32),
                   jax.ShapeDtypeStruct((t, LANES), jnp.int32), jax.ShapeDtypeStruct((t, LANES), F32)],
        compiler_params=_cparams("parallel"),
        name="outproj_router",
    )(oa, orec, x, ga, scf, shf, gnw, fnw, wo, rw, rb)


def _gather_kernel(idx_ref, src_ref, dst_ref, sem, *, rows_per_step):
    def row_copy(r, src_row):
        return pltpu.make_async_copy(src_ref.at[src_row], dst_ref.at[r], sem)

    def issue(r, carry):
        row_copy(r, idx_ref[r]).start()
        return carry

    def drain(r, carry):
        row_copy(r, 0).wait()
        return carry

    lax.fori_loop(0, rows_per_step, issue, 0)
    lax.fori_loop(0, rows_per_step, drain, 0)


def _gather_rows(idx, src, *, rows_per_step):
    n = idx.shape[0]
    return pl.pallas_call(
        functools.partial(_gather_kernel, rows_per_step=rows_per_step),
        grid=(n // rows_per_step,),
        in_specs=[pl.BlockSpec((rows_per_step,), lambda i: (i,), memory_space=pltpu.SMEM),
                  pl.BlockSpec(memory_space=pl.ANY)],
        out_specs=pl.BlockSpec((rows_per_step,) + src.shape[1:], lambda i: (i, 0, 0)),
        out_shape=jax.ShapeDtypeStruct((n,) + src.shape[1:], src.dtype),
        scratch_shapes=[pltpu.SemaphoreType.DMA(())],
        compiler_params=_cparams("parallel"),
        name="gather_rows",
    )(idx, src)


def _moe_kernel(te_ref, nu_ref, x_ref, w1g_ref, w1l_ref, b1g_ref, b1l_ref, w2_ref, b2_ref, o_ref,
                xb_scr, acc_scr, *, nseg):
    i = pl.program_id(0)
    j = pl.program_id(1)
    tm = xb_scr.shape[0]

    @pl.when(i < nu_ref[0])
    def _():
        @pl.when(j == 0)
        def _():
            for s in range(nseg):
                xb_scr[:, s * LANES:(s + 1) * LANES] = x_ref[pl.ds(s, tm, stride=nseg), :].astype(BF16)
            acc_scr[...] = jnp.zeros(acc_scr.shape, F32)

        xb = xb_scr[...]
        yg = _dot(xb, w1g_ref[...].astype(BF16)) + b1g_ref[...]
        yl = _dot(xb, w1l_ref[...].astype(BF16)) + b1l_ref[...]
        glu = jnp.minimum(yg, SWIGLU_LIMIT)
        lin = jnp.clip(yl, -SWIGLU_LIMIT, SWIGLU_LIMIT)
        a = glu * _sigmoid(SWIGLU_ALPHA * glu) * (lin + 1.0)
        acc_scr[...] += _dot(a.astype(BF16), w2_ref[...].astype(BF16))

        @pl.when(j == pl.num_programs(1) - 1)
        def _():
            for s in range(nseg):
                seg = slice(s * LANES, (s + 1) * LANES)
                o_ref[pl.ds(s, tm, stride=nseg), :] = acc_scr[:, seg] + b2_ref[:, seg]

    @pl.when(jnp.logical_and(i >= nu_ref[0], j == 0))
    def _():
        o_ref[...] = jnp.zeros(o_ref.shape, F32)


def _moe_ffn(tile_e, n_used, xs2, w1, b1, w2, b2, *, tm, th, nseg):
    d = nseg * LANES
    p = xs2.shape[0] // nseg
    ne, _, two_de = w1.shape
    de = two_de // 2
    nj = de // th
    ntiles = p // tm

    def tile(i, nu):
        return jnp.minimum(i, nu[0] - 1)

    def hid(i, j, nu):
        return jnp.where(i < nu[0], j, nj - 1)

    grid_spec = pltpu.PrefetchScalarGridSpec(
        num_scalar_prefetch=2,
        grid=(ntiles, nj),
        in_specs=[pl.BlockSpec((tm * nseg, LANES), lambda i, j, te, nu: (tile(i, nu), 0)),
                  pl.BlockSpec((None, d, th), lambda i, j, te, nu: (te[tile(i, nu)], 0, hid(i, j, nu))),
                  pl.BlockSpec((None, d, th), lambda i, j, te, nu: (te[tile(i, nu)], 0, nj + hid(i, j, nu))),
                  pl.BlockSpec((None, 1, th), lambda i, j, te, nu: (te[tile(i, nu)], 0, hid(i, j, nu))),
                  pl.BlockSpec((None, 1, th), lambda i, j, te, nu: (te[tile(i, nu)], 0, nj + hid(i, j, nu))),
                  pl.BlockSpec((None, th, d), lambda i, j, te, nu: (te[tile(i, nu)], hid(i, j, nu), 0)),
                  pl.BlockSpec((None, 1, d), lambda i, j, te, nu: (te[tile(i, nu)], 0, 0))],
        out_specs=pl.BlockSpec((tm * nseg, LANES), lambda i, j, te, nu: (i, 0)),
        scratch_shapes=[pltpu.VMEM((tm, d), BF16), pltpu.VMEM((tm, d), F32)])
    return pl.pallas_call(
        functools.partial(_moe_kernel, nseg=nseg),
        grid_spec=grid_spec,
        out_shape=jax.ShapeDtypeStruct((p * nseg, LANES), F32),
        compiler_params=_cparams("arbitrary", "arbitrary"),
        name="moe_experts",
    )(tile_e, n_used, xs2, w1, w1, b1.reshape(ne, 1, two_de), b1.reshape(ne, 1, two_de), w2,
      b2.reshape(ne, 1, d))


def _combine_kernel(*refs, nseg):
    y_refs = refs[:TOP_K]
    gate_ref, x1_ref, gf_ref, o_ref = refs[TOP_K:]
    tm = x1_ref.shape[0]
    gates = gate_ref[...]
    gk = [gates[:, kk:kk + 1] for kk in range(TOP_K)]
    for s in range(nseg):
        seg = slice(s * LANES, (s + 1) * LANES)
        acc = gk[0] * y_refs[0][pl.ds(s, tm, stride=nseg), :]
        for kk in range(1, TOP_K):
            acc = acc + gk[kk] * y_refs[kk][pl.ds(s, tm, stride=nseg), :]
        o_ref[:, seg] = x1_ref[:, seg] + gf_ref[:, seg] * acc


def _combine(yg2, gates, x1, gf, mod_spec, *, tm, row_off, nblk_all, nseg):
    t, d = x1.shape

    def ymap(kk):
        return lambda i: (kk * nblk_all + row_off + i, 0)

    return pl.pallas_call(
        functools.partial(_combine_kernel, nseg=nseg),
        grid=(t // tm,),
        in_specs=[pl.BlockSpec((tm * nseg, LANES), ymap(kk)) for kk in range(TOP_K)] + [
            pl.BlockSpec((tm, LANES), lambda i: (i + row_off, 0)),
            pl.BlockSpec((tm, d), lambda i: (i, 0)),
            mod_spec],
        out_specs=pl.BlockSpec((tm, d), lambda i: (i, 0)),
        out_shape=jax.ShapeDtypeStruct((t, d), F32),
        compiler_params=_cparams("parallel"),
        name="moe_combine",
    )(*([yg2] * TOP_K), gates, x1, gf)


def _rope_angles(pos, half):
    inv = ROPE_THETA ** (-jnp.arange(half, dtype=F32) / half)
    ang = pos.astype(F32)[:, None] * inv[None, :]
    return jnp.cos(ang), jnp.sin(ang)


def _pad_lanes(v, width=LANES):
    v = v.reshape(1, -1).astype(F32)
    return jnp.pad(v, ((0, 0), (0, width - v.shape[1])))


def _swap_halves(v):
    h = v.shape[-1] // 2
    return jnp.concatenate([v[..., h:], v[..., :h]], -1)


def kernel(x_prompt, x_sample, cache_mla, state_hgrn, page_table, c_prompt, c_sample, w_ada, b_ada,
           attn_norm_w, w_in, q_a_norm_w, w_uq, kv_a_norm_w, w_ukv, q_head_norm_w, k_head_norm_w,
           attn_group_norm_w, hg_lower_bound, hg_norm_w, w_out, ffn_norm_w, router_w, router_b,
           w1, b1, w2, b2):
    nb_p, seq, d = x_prompt.shape
    nb_s, ntok, _ = x_sample.shape
    depth = w_ada.shape[0]
    assert depth == 1, "single-layer step"
    page = cache_mla.shape[2]
    cache_w = cache_mla.shape[3]
    n_pages = page_table.shape[1]
    past_len = n_pages * page
    q_lora, nheads, d_head = w_uq.shape[1], w_uq.shape[2], w_uq.shape[3]
    kv_lora = w_ukv.shape[1]
    rope = cache_w - kv_lora
    nope = d_head - rope
    v_head = w_ukv.shape[3] - nope
    hg_heads, hg_dk, hg_dv = state_hgrn.shape[2], state_hgrn.shape[3], state_hgrn.shape[4]
    n_experts = router_w.shape[2]
    assert nope == LANES and v_head == LANES and hg_dk == LANES and hg_dv == LANES
    assert 2 * rope == LANES and n_experts <= LANES and d % LANES == 0
    half = rope // 2
    nseg = d // LANES
    attn_scale = d_head ** -0.5
    t_p = nb_p * seq
    t_s = nb_s * ntok

    mod = _adaln(jnp.concatenate([c_prompt, c_sample], 0), w_ada[0], b_ada[0])
    mods = [mod[:, i * d:(i + 1) * d] for i in range(6)]
    mods_p = [m[:nb_p].reshape(nb_p, 1, d) for m in mods]
    mods_s = [jnp.repeat(m[nb_p:], ntok, axis=0) for m in mods]

    wi = w_in[0]
    o_kr = q_lora + kv_lora
    w_kr = wi[:, o_kr:o_kr + rope]
    zpad = jnp.zeros((d, LANES - rope), F32)
    wa = jnp.concatenate([wi[:, :o_kr], w_kr, zpad, _swap_halves(w_kr), zpad], 1).astype(BF16)
    wh = wi[:, o_kr + rope:].astype(BF16)
    wq_full = jnp.transpose(w_uq[0], (1, 0, 2))
    wq_r = wq_full[:, :, nope:]
    wq_rs = _swap_halves(wq_r)
    wq = jnp.concatenate([wq_full[:, :, :nope], wq_r, wq_rs, wq_rs, wq_r], -1).astype(BF16)
    wkv = jnp.transpose(w_ukv[0], (1, 0, 2)).astype(BF16)
    wuk = wkv[:, :, :nope]
    wuv = wkv[:, :, nope:]
    wukt = jnp.transpose(wuk, (0, 2, 1)).reshape(nheads * nope, kv_lora)
    qn_w, kn_w = q_head_norm_w[0], k_head_norm_w[0]
    qr_w, kr_w = qn_w[nope:], kn_w[nope:]
    qw_n = _pad_lanes(qn_w[:nope])
    qw_a = jnp.concatenate([qr_w, _swap_halves(qr_w)]).reshape(1, LANES)
    qw_b = jnp.concatenate([_swap_halves(qr_w), qr_w]).reshape(1, LANES)
    kw_n, kw_a, kw_b = _pad_lanes(kn_w[:nope]), _pad_lanes(kr_w), _pad_lanes(_swap_halves(kr_w))
    kw_rr = jnp.concatenate([kr_w, kr_w]).reshape(1, LANES)
    nw = attn_norm_w[0].reshape(1, d)
    qaw = q_a_norm_w[0].reshape(1, q_lora)
    kvw = kv_a_norm_w[0].reshape(1, kv_lora)

    def token_tables(pos):
        cos, sin = _rope_angles(pos, half)
        z = jnp.zeros((pos.shape[0], LANES - rope), F32)
        k_cos = jnp.concatenate([cos, cos, z], -1)
        k_sin = jnp.concatenate([-sin, sin, z], -1)
        q_cos = jnp.concatenate([cos, cos, cos, -cos], -1)
        q_sin = jnp.concatenate([-sin, sin, sin, sin], -1)
        return k_cos, k_sin, q_cos, q_sin

    tabs_p = token_tables(jnp.arange(seq))
    tabs_s = token_tables(past_len + (jnp.arange(t_s) % ntok))

    tm_p = min(512, seq)
    tm_s = t_s
    nblk = seq // tm_p
    modspec_p2 = pl.BlockSpec((None, 1, d), lambda i, j: (i // nblk, 0, 0))
    modspec_s2 = pl.BlockSpec((tm_s, d), lambda i, j: (i, 0))
    dims = dict(q_lora=q_lora, kv_lora=kv_lora, rope=rope)

    xp = x_prompt.reshape(t_p, d)
    xs = x_sample.reshape(t_s, d)
    cqn_p, rows_p, latb_p, krr_p, krsq_p, zh_p = _inproj(
        xp, mods_p[1], mods_p[0], modspec_p2, nw, wa, wh, qaw, kvw, kw_a, kw_b, tabs_p[0], tabs_p[1],
        lambda i, j: (i % nblk, 0), tm=tm_p, **dims)
    cqn_s, rows_s, latb_s, krr_s, krsq_s, zh_s = _inproj(
        xs, mods_s[1], mods_s[0], modspec_s2, nw, wa, wh, qaw, kvw, kw_a, kw_b, tabs_s[0], tabs_s[1],
        lambda i, j: (i, 0), tm=tm_s, **dims)

    q_p = _queries(cqn_p, wq, qw_n, qw_a, qw_b, tabs_p[2:], lambda i, h: (i % nblk, 0), tm=tm_p,
                   scale=attn_scale, d_head=d_head)
    k_p, v_p = _keys_values(latb_p, krr_p, krsq_p, wkv, kw_n, tm=tm_p, d_head=d_head)
    oa_p = _flash(q_p, k_p, v_p, nbatch=nb_p, seq=seq, tq=tm_p)

    q_s = _queries(cqn_s, wq, qw_n, qw_a, qw_b, tabs_s[2:], lambda i, h: (i, 0), tm=tm_s,
                   scale=attn_scale, d_head=d_head)
    qf = _absorb(q_s, wuk, kw_n, kw_rr, kv_lora=kv_lora)
    qf = qf.reshape(nb_s, ntok * nheads, kv_lora + LANES)

    def cs_table(pos):
        cos, sin = _rope_angles(pos, half)
        return jnp.concatenate([cos, cos, sin, sin], -1).T

    cs_pages = cs_table(jnp.arange(past_len))
    cs_new = cs_table(past_len + jnp.arange(page))
    cache_t = jnp.swapaxes(cache_mla, 2, 3)
    new_pages = jnp.zeros((nb_s, page, cache_w), F32).at[:, :ntok].set(rows_s.reshape(nb_s, ntok, cache_w))
    new_pages_t = jnp.swapaxes(new_pages, 1, 2)
    n_group = 8
    while n_pages % n_group:
        n_group //= 2
    lat_mix = _sample_attention(page_table, cache_t, new_pages_t, qf, wukt, cs_pages, cs_new,
                                n_group=n_group, kv_lora=kv_lora, rope=rope, nheads=nheads, ntok=ntok,
                                d_head=d_head)
    lat_mix = jnp.swapaxes(lat_mix, 1, 2)
    oa_s = _uv_project(lat_mix.reshape(t_s, nheads * kv_lora), wuv)

    gnw_h = hg_norm_w[0].reshape(1, hg_dv)
    chunk_p = min(HG_CHUNK, seq)
    orec_p, st_p = _hgrn(zh_p.reshape(nb_p, seq, -1), hg_lower_bound, None, gnw_h, nheads=hg_heads, hps=1,
                         tblock=min(256, seq), chunk=chunk_p, sub=min(HG_SUB, chunk_p), t_valid=None)
    t_pad = -(-ntok // SUBLANES) * SUBLANES
    zh_s3 = jnp.pad(zh_s.reshape(nb_s, ntok, -1), ((0, 0), (0, t_pad - ntok), (0, 0)))
    orec_s, st_s = _hgrn(zh_s3, hg_lower_bound, state_hgrn[0], gnw_h, nheads=hg_heads, hps=hg_heads,
                         tblock=t_pad, chunk=t_pad, sub=t_pad, t_valid=ntok)
    orec_s = orec_s[:, :ntok].reshape(t_s, -1)

    wo = w_out[0].astype(BF16)
    rw = jnp.pad(router_w[0], ((0, 0), (0, LANES - n_experts)))
    rb = jnp.concatenate([router_b[0], jnp.full((LANES - n_experts,), NEG_INF, F32)]).reshape(1, LANES)
    gnw_a = attn_group_norm_w[0].reshape(1, -1)
    fnw = ffn_norm_w[0].reshape(1, d)
    modspec_p1 = pl.BlockSpec((None, 1, d), lambda i: (i // nblk, 0, 0))
    modspec_s1 = pl.BlockSpec((tm_s, d), lambda i: (i, 0))
    x1_p, h2_p, ti_p, gt_p = _outproj(oa_p, orec_p.reshape(t_p, -1), xp, mods_p[2], mods_p[4], mods_p[3],
                                      modspec_p1, gnw_a, fnw, wo, rw, rb, tm=tm_p)
    x1_s, h2_s, ti_s, gt_s = _outproj(oa_s, orec_s, xs, mods_s[2], mods_s[4], mods_s[3],
                                      modspec_s1, gnw_a, fnw, wo, rw, rb, tm=tm_s)

    n_tok = t_p + t_s
    h2 = jnp.concatenate([h2_p, h2_s], 0).reshape(n_tok, nseg, LANES)
    top_i = jnp.concatenate([ti_p[:, :TOP_K], ti_s[:, :TOP_K]], 0)
    gates = jnp.concatenate([gt_p, gt_s], 0)
    tm_e = min(512, n_tok)
    rows_per_step = 1024
    n_assign = n_tok * TOP_K
    flat_e = top_i.reshape(-1)
    onehot = (flat_e[:, None] == jnp.arange(n_experts, dtype=jnp.int32)[None, :]).astype(jnp.int32)
    csum = jnp.cumsum(onehot, axis=0)
    counts = csum[-1]
    rank = jnp.take_along_axis(csum, flat_e[:, None], axis=1)[:, 0] - 1
    padded = (counts + tm_e - 1) // tm_e * tm_e
    pend = jnp.cumsum(padded)
    pstart = pend - padded
    dest = (pstart[flat_e] + rank).astype(jnp.int32)
    n_slots = -(-(n_assign + n_experts * tm_e) // rows_per_step) * rows_per_step
    n_slots = -(-n_slots // tm_e) * tm_e
    flat_tok = jnp.arange(n_assign, dtype=jnp.int32) // TOP_K
    slot_tok = jnp.zeros((n_slots,), jnp.int32).at[dest].set(flat_tok)
    n_tiles = n_slots // tm_e
    tile_e = jnp.minimum(jnp.searchsorted(pend, jnp.arange(n_tiles, dtype=jnp.int32) * tm_e, side='right'),
                         n_experts - 1).astype(jnp.int32)
    n_used = (pend[-1] // tm_e).astype(jnp.int32).reshape(1)

    xs_sorted = _gather_rows(slot_tok, h2, rows_per_step=rows_per_step)
    ys2 = _moe_ffn(tile_e, n_used, xs_sorted.reshape(n_slots * nseg, LANES), w1[0], b1[0], w2[0], b2[0],
                   tm=tm_e, th=min(256, d), nseg=nseg)
    tm_c = min(256, t_s)
    assert n_tok % tm_c == 0 and t_p % tm_c == 0
    dest_kmajor = dest.reshape(n_tok, TOP_K).T.reshape(-1)
    n_pad = -(-n_assign // rows_per_step) * rows_per_step
    dest_kmajor = jnp.pad(dest_kmajor, (0, n_pad - n_assign))
    yg = _gather_rows(dest_kmajor, ys2.reshape(n_slots, nseg, LANES), rows_per_step=rows_per_step)
    yg2 = yg.reshape(n_pad * nseg, LANES)

    nblk_all = n_tok // tm_c
    y_p = _combine(yg2, gates, x1_p, mods_p[5], pl.BlockSpec((None, 1, d), lambda i: (i // (seq // tm_c), 0, 0)),
                   tm=tm_c, row_off=0, nblk_all=nblk_all, nseg=nseg)
    y_s = _combine(yg2, gates, x1_s, mods_s[5], pl.BlockSpec((tm_c, d), lambda i: (i, 0)),
                   tm=tm_c, row_off=t_p // tm_c, nblk_all=nblk_all, nseg=nseg)

    return (y_p.reshape(nb_p, seq, d), y_s.reshape(nb_s, ntok, d),
            rows_p.reshape(1, nb_p, seq, cache_w), rows_s.reshape(1, nb_s, ntok, cache_w),
            st_p[None], st_s[None])
```

```python
import functools

import jax
import jax.numpy as jnp
from jax import lax
from jax.experimental import pallas as pl
from jax.experimental.pallas import tpu as pltpu

F32 = jnp.float32
BF16 = jnp.bfloat16

NORM_EPS = 1e-6
NEG_INF = -1e30
ROPE_THETA = 10000.0
TOP_K = 4
SWIGLU_LIMIT = 7.0
SWIGLU_ALPHA = 1.702
HG_CHUNK = 64
HG_SUB = 16
LANES = 128
SUBLANES = 8
MXU_WIDTH = 256
VMEM_LIMIT_BYTES = 56 * 1024 * 1024


def _cparams(*sem):
    return pltpu.CompilerParams(dimension_semantics=sem, vmem_limit_bytes=VMEM_LIMIT_BYTES)


def _pick(n, pref, mult=LANES):
    if n <= pref:
        return n
    t = pref - pref % mult
    while t > mult and n % t:
        t -= mult
    assert n % t == 0, (n, pref, mult)
    return t


def _sigmoid(x):
    return 1.0 / (1.0 + jnp.exp(-x))


def _rms(x, eps=NORM_EPS):
    return x * lax.rsqrt(jnp.mean(x * x, axis=-1, keepdims=True) + eps)


def _dot(a, b):
    return jnp.dot(a, b, preferred_element_type=F32)


def _dot_nt(a, b):
    return lax.dot_general(a, b, (((1,), (1,)), ((), ())), preferred_element_type=F32)


def _dot_tn(a, b):
    return lax.dot_general(a, b, (((0,), (0,)), ((), ())), preferred_element_type=F32)


def _adaln_kernel(c_ref, w_ref, b_ref, o_ref):
    c = c_ref[...]
    a = (c * _sigmoid(c)).astype(BF16)
    o_ref[...] = _dot(a, w_ref[...].astype(BF16)) + b_ref[...]


def _adaln(c, w, b):
    n, d = c.shape
    nout = w.shape[1]
    tn = _pick(nout, 1024)
    return pl.pallas_call(
        _adaln_kernel,
        grid=(nout // tn,),
        in_specs=[pl.BlockSpec((n, d), lambda j: (0, 0)),
                  pl.BlockSpec((d, tn), lambda j: (0, j)),
                  pl.BlockSpec((1, tn), lambda j: (0, j))],
        out_specs=pl.BlockSpec((n, tn), lambda j: (0, j)),
        out_shape=jax.ShapeDtypeStruct((n, nout), F32),
        compiler_params=_cparams("parallel"),
        name="adaln",
    )(c, w, b.reshape(1, nout))


def _inproj_kernel(x_ref, sc_ref, sh_ref, nw_ref, wa_ref, wh_ref, qaw_ref, kvw_ref, wkr_ref, wkrs_ref,
                   cos_ref, sin_ref,
                   cqn_ref, rows_ref, latb_ref, krr_ref, krsq_ref, zh_ref, h_scr, *, q_lora, kv_lora, rope):
    j = pl.program_id(1)

    @pl.when(j == 0)
    def _():
        h = _rms(x_ref[...]) * nw_ref[...]
        h = h * (1.0 + sc_ref[...]) + sh_ref[...]
        hb = h.astype(BF16)
        h_scr[...] = hb
        za = _dot(hb, wa_ref[...])
        cqn_ref[...] = (_rms(za[:, :q_lora]) * qaw_ref[...]).astype(BF16)
        lat = _rms(za[:, q_lora:q_lora + kv_lora]) * kvw_ref[...]
        o = q_lora + kv_lora
        kr_a = za[:, o:o + LANES]
        kr_b = za[:, o + LANES:o + 2 * LANES]
        rows_ref[:, :kv_lora] = lat
        rows_ref[:, kv_lora:] = kr_a[:, :rope]
        latb_ref[...] = lat.astype(BF16)
        krr_ref[...] = kr_a * wkr_ref[...] * cos_ref[...] + kr_b * wkrs_ref[...] * sin_ref[...]
        krsq_ref[...] = jnp.broadcast_to(jnp.sum(kr_a * kr_a, axis=-1, keepdims=True), krsq_ref.shape)

    @pl.when(j > 0)
    def _():
        zh_ref[...] = _dot(h_scr[...], wh_ref[...])


def _inproj(x, sc, sh, mod_spec, nw, wa, wh, qaw, kvw, wkr, wkrs, cos_t, sin_t, tab_map, *, tm,
            q_lora, kv_lora, rope):
    t, d = x.shape
    na = wa.shape[1]
    nh = wh.shape[1]
    tn = _pick(nh, 1024)
    nj = nh // tn
    const = lambda i, j: (0, 0)
    row = lambda i, j: (i, 0)
    kern = functools.partial(_inproj_kernel, q_lora=q_lora, kv_lora=kv_lora, rope=rope)
    return pl.pallas_call(
        kern,
        grid=(t // tm, 1 + nj),
        in_specs=[pl.BlockSpec((tm, d), row), mod_spec, mod_spec,
                  pl.BlockSpec((1, d), const),
                  pl.BlockSpec((d, na), const),
                  pl.BlockSpec((d, tn), lambda i, j: (0, jnp.maximum(j - 1, 0))),
                  pl.BlockSpec((1, q_lora), const), pl.BlockSpec((1, kv_lora), const),
                  pl.BlockSpec((1, LANES), const), pl.BlockSpec((1, LANES), const),
                  pl.BlockSpec((tm, LANES), tab_map), pl.BlockSpec((tm, LANES), tab_map)],
        out_specs=[pl.BlockSpec((tm, q_lora), row),
                   pl.BlockSpec((tm, kv_lora + rope), row),
                   pl.BlockSpec((tm, kv_lora), row),
                   pl.BlockSpec((tm, LANES), row),
                   pl.BlockSpec((tm, LANES), row),
                   pl.BlockSpec((tm, tn), lambda i, j: (i, jnp.maximum(j - 1, 0)))],
        out_shape=[jax.ShapeDtypeStruct((t, q_lora), BF16),
                   jax.ShapeDtypeStruct((t, kv_lora + rope), F32),
                   jax.ShapeDtypeStruct((t, kv_lora), BF16),
                   jax.ShapeDtypeStruct((t, LANES), F32),
                   jax.ShapeDtypeStruct((t, LANES), F32),
                   jax.ShapeDtypeStruct((t, nh), F32)],
        scratch_shapes=[pltpu.VMEM((tm, d), BF16)],
        compiler_params=_cparams("parallel", "arbitrary"),
        name="inproj",
    )(x, sc, sh, nw, wa, wh, qaw, kvw, wkr, wkrs, cos_t, sin_t)


def _q_kernel(cqn_ref, w_ref, wn_ref, wa_ref, wb_ref, c_ref, s_ref, o_ref, *, scale, d_head):
    y = _dot(cqn_ref[...], w_ref[...])
    nope = y[:, :LANES]
    a = y[:, LANES:2 * LANES]
    b = y[:, 2 * LANES:]
    ssq = jnp.sum(nope * nope, axis=-1, keepdims=True) + 0.5 * jnp.sum(a * a, axis=-1, keepdims=True)
    rinv = lax.rsqrt(ssq * (1.0 / d_head) + NORM_EPS) * scale
    o_ref[:, :LANES] = (nope * rinv * wn_ref[...]).astype(o_ref.dtype)
    rot = a * wa_ref[...] * c_ref[...] + b * wb_ref[...] * s_ref[...]
    o_ref[:, LANES:] = (rot * rinv).astype(o_ref.dtype)


def _queries(cqn, wq, wn, wa, wb, tabs, tab_map, *, tm, scale, d_head):
    t, r = cqn.shape
    nheads = wq.shape[0]
    wout = 2 * LANES
    const = lambda i, h: (0, 0)
    kern = functools.partial(_q_kernel, scale=scale, d_head=d_head)
    tab_spec = pl.BlockSpec((tm, LANES), lambda i, h: tab_map(i, h))
    return pl.pallas_call(
        kern,
        grid=(t // tm, nheads),
        in_specs=[pl.BlockSpec((tm, r), lambda i, h: (i, 0)),
                  pl.BlockSpec((None, r, 3 * LANES), lambda i, h: (h, 0, 0)),
                  pl.BlockSpec((1, LANES), const), pl.BlockSpec((1, LANES), const),
                  pl.BlockSpec((1, LANES), const),
                  tab_spec, tab_spec],
        out_specs=pl.BlockSpec((None, tm, wout), lambda i, h: (h, i, 0)),
        out_shape=jax.ShapeDtypeStruct((nheads, t, wout), BF16),
        compiler_params=_cparams("parallel", "parallel"),
        name="queries",
    )(cqn, wq, wn, wa, wb, *tabs)


def _kv_kernel(latb_ref, krr_ref, krsq_ref, w_ref, wn_ref, k_ref, v_ref, *, d_head):
    y = _dot(latb_ref[...], w_ref[...])
    kn = y[:, :LANES]
    ssq = jnp.sum(kn * kn, axis=-1, keepdims=True) + krsq_ref[:, :1]
    rinv = lax.rsqrt(ssq * (1.0 / d_head) + NORM_EPS)
    k_ref[:, :LANES] = (kn * rinv * wn_ref[...]).astype(BF16)
    k_ref[:, LANES:] = (krr_ref[...] * rinv).astype(BF16)
    v_ref[...] = y[:, LANES:].astype(BF16)


def _keys_values(latb, krr, krsq, wkv, wn, *, tm, d_head):
    t, c = latb.shape
    nheads = wkv.shape[0]
    row = lambda i, h: (i, 0)
    return pl.pallas_call(
        functools.partial(_kv_kernel, d_head=d_head),
        grid=(t // tm, nheads),
        in_specs=[pl.BlockSpec((tm, c), row), pl.BlockSpec((tm, LANES), row), pl.BlockSpec((tm, LANES), row),
                  pl.BlockSpec((None, c, 2 * LANES), lambda i, h: (h, 0, 0)),
                  pl.BlockSpec((1, LANES), lambda i, h: (0, 0))],
        out_specs=[pl.BlockSpec((None, tm, 2 * LANES), lambda i, h: (h, i, 0)),
                   pl.BlockSpec((None, tm, LANES), lambda i, h: (h, i, 0))],
        out_shape=[jax.ShapeDtypeStruct((nheads, t, 2 * LANES), BF16),
                   jax.ShapeDtypeStruct((nheads, t, LANES), BF16)],
        compiler_params=_cparams("parallel", "parallel"),
        name="keys_values",
    )(latb, krr, krsq, wkv, wn)


def _flash_kernel(q_ref, k_ref, v_ref, o_ref, m_scr, l_scr, acc_scr, *, tq):
    qi = pl.program_id(2)
    ki = pl.program_id(3)

    @pl.when(ki == 0)
    def _():
        m_scr[...] = jnp.full(m_scr.shape, NEG_INF, F32)
        l_scr[...] = jnp.zeros(l_scr.shape, F32)
        acc_scr[...] = jnp.zeros(acc_scr.shape, F32)

    def block(on_diagonal):
        s = _dot_nt(q_ref[...], k_ref[...])
        if on_diagonal:
            qpos = lax.broadcasted_iota(jnp.int32, s.shape, 0)
            kpos = lax.broadcasted_iota(jnp.int32, s.shape, 1)
            s = jnp.where(kpos <= qpos, s, NEG_INF)
        m_old = m_scr[...]
        m_new = jnp.maximum(m_old, jnp.max(s, axis=-1, keepdims=True))
        alpha = jnp.exp(m_old - m_new)
        p = jnp.exp(s - m_new)
        l_scr[...] = alpha * l_scr[...] + jnp.sum(p, axis=-1, keepdims=True)
        acc_scr[...] = alpha * acc_scr[...] + _dot(p.astype(BF16), v_ref[...])
        m_scr[...] = m_new

    pl.when(ki < qi)(functools.partial(block, False))
    pl.when(ki == qi)(functools.partial(block, True))

    @pl.when(ki == pl.num_programs(3) - 1)
    def _():
        o_ref[...] = acc_scr[...] / l_scr[...]


def _flash(q, k, v, *, nbatch, seq, tq):
    nheads, t, dq = q.shape
    dv = v.shape[-1]
    nq = seq // tq
    return pl.pallas_call(
        functools.partial(_flash_kernel, tq=tq),
        grid=(nbatch, nheads, nq, nq),
        in_specs=[pl.BlockSpec((None, tq, dq), lambda b, h, qi, ki: (h, b * nq + qi, 0)),
                  pl.BlockSpec((None, tq, dq), lambda b, h, qi, ki: (h, b * nq + jnp.minimum(ki, qi), 0)),
                  pl.BlockSpec((None, tq, dv), lambda b, h, qi, ki: (h, b * nq + jnp.minimum(ki, qi), 0))],
        out_specs=pl.BlockSpec((tq, dv), lambda b, h, qi, ki: (b * nq + qi, h)),
        out_shape=jax.ShapeDtypeStruct((t, nheads * dv), F32),
        scratch_shapes=[pltpu.VMEM((tq, 1), F32), pltpu.VMEM((tq, 1), F32), pltpu.VMEM((tq, dv), F32)],
        compiler_params=_cparams("parallel", "parallel", "parallel", "arbitrary"),
        name="prompt_attention",
    )(q, k, v)


def _absorb_kernel(q_ref, wuk_ref, wkn_ref, wkr_ref, o_ref, *, kv_lora):
    q = q_ref[...].astype(F32)
    qn = (q[:, :LANES] * wkn_ref[...]).astype(BF16)
    o_ref[:, :kv_lora] = _dot_nt(qn, wuk_ref[...]).astype(BF16)
    o_ref[:, kv_lora:] = (q[:, LANES:] * wkr_ref[...]).astype(BF16)


def _absorb(q, wuk, wkn, wkr, *, kv_lora):
    nheads, t, wq = q.shape
    wout = kv_lora + LANES
    const = lambda h: (0, 0)
    return pl.pallas_call(
        functools.partial(_absorb_kernel, kv_lora=kv_lora),
        grid=(nheads,),
        in_specs=[pl.BlockSpec((None, t, wq), lambda h: (h, 0, 0)),
                  pl.BlockSpec((None, kv_lora, LANES), lambda h: (h, 0, 0)),
                  pl.BlockSpec((1, LANES), const), pl.BlockSpec((1, LANES), const)],
        out_specs=pl.BlockSpec((t, wout), lambda h: (0, h)),
        out_shape=jax.ShapeDtypeStruct((t, nheads * wout), BF16),
        compiler_params=_cparams("parallel"),
        name="absorb_queries",
    )(q, wuk, wkn, wkr)


def _sattn_kernel(pt_ref, *refs, n_group, kv_lora, rope, nheads, ntok, d_head, page):
    page_refs = refs[:n_group]
    (new_ref, qf_ref, wukt_ref, cs_ref, csn_ref, o_ref,
     lhs_scr, latb_scr, m_scr, l_scr, acc_scr) = refs[n_group:]
    step = pl.program_id(1)
    nrow = nheads * ntok
    nk_w = wukt_ref.shape[0]

    @pl.when(step == 0)
    def _():
        lhs_scr[:nk_w, :] = wukt_ref[...]
        lhs_scr[nk_w:, :] = qf_ref[:, :kv_lora]
        m_scr[...] = jnp.full(m_scr.shape, NEG_INF, F32)
        l_scr[...] = jnp.zeros(l_scr.shape, F32)
        acc_scr[...] = jnp.zeros(acc_scr.shape, F32)

    eye = (lax.broadcasted_iota(jnp.int32, (nrow, nrow), 0)
           == lax.broadcasted_iota(jnp.int32, (nrow, nrow), 1))

    def as_row(col):
        return jnp.sum(jnp.where(eye, col, 0.0), axis=0, keepdims=True)

    def scores(lat_b, kr_t, cs):
        nk = lat_b.shape[1]
        r = _dot(lhs_scr[...], lat_b)
        kn = r[:nk_w]
        kn2 = jnp.sum((kn * kn).reshape(nheads, nk_w // nheads, nk), axis=1)
        krsq = jnp.sum(kr_t * kr_t, axis=0, keepdims=True)
        rinv = lax.rsqrt((kn2 + krsq) * (1.0 / d_head) + NORM_EPS)
        rinv_rows = jnp.concatenate([rinv] * ntok, axis=0)
        f2 = (jnp.concatenate([kr_t, kr_t], axis=0) * cs).astype(BF16)
        return (r[nk_w:] + _dot(qf_ref[:, kv_lora:], f2)) * rinv_rows

    def update(s, lat_b):
        m_old = m_scr[...]
        m_new = jnp.maximum(m_old, jnp.max(s, axis=-1, keepdims=True))
        alpha = jnp.exp(m_old - m_new)
        p = jnp.exp(s - m_new)
        l_scr[...] = alpha * l_scr[...] + jnp.sum(p, axis=-1, keepdims=True)
        acc_scr[...] = acc_scr[...] * as_row(alpha) + _dot_nt(lat_b, p.astype(BF16))
        m_scr[...] = m_new

    per = max(1, min(n_group, MXU_WIDTH // page))
    s_parts = []
    for g in range(0, n_group, per):
        cols = slice(g * page, (g + per) * page)
        lat_b = jnp.concatenate([page_refs[g + u][:kv_lora, :] for u in range(per)], axis=1).astype(BF16)
        kr_t = jnp.concatenate([page_refs[g + u][kv_lora:, :] for u in range(per)], axis=1)
        latb_scr[:, cols] = lat_b
        s_parts.append(scores(lat_b, kr_t, cs_ref[:, cols]))
    update(jnp.concatenate(s_parts, axis=1), latb_scr[...])

    @pl.when(step == pl.num_programs(1) - 1)
    def _():
        key = lax.broadcasted_iota(jnp.int32, (nrow, page), 1)
        tok = lax.broadcasted_iota(jnp.int32, (nrow, page), 0) // nheads
        lat_b = new_ref[:kv_lora, :].astype(BF16)
        s_new = scores(lat_b, new_ref[kv_lora:, :], csn_ref[...])
        update(jnp.where(key <= tok, s_new, NEG_INF), lat_b)
        o_ref[...] = acc_scr[...] / as_row(l_scr[...])


def _sample_attention(page_table, cache_t, new_pages_t, qf, wukt, cs_pages, cs_new, *,
                      n_group, kv_lora, rope, nheads, ntok, d_head):
    nb, n_pages = page_table.shape
    cw, page = cache_t.shape[2], cache_t.shape[3]
    nrow = nheads * ntok
    wqf = qf.shape[-1]
    nsteps = n_pages // n_group

    def page_map(g):
        return lambda b, s, pt: (0, pt[b * n_pages + s * n_group + g], 0, 0)

    kern = functools.partial(_sattn_kernel, n_group=n_group, kv_lora=kv_lora, rope=rope, nheads=nheads,
                             ntok=ntok, d_head=d_head, page=page)
    in_specs = [pl.BlockSpec((None, None, cw, page), page_map(g)) for g in range(n_group)]
    in_specs += [pl.BlockSpec((None, cw, page), lambda b, s, pt: (b, 0, 0)),
                 pl.BlockSpec((None, nrow, wqf), lambda b, s, pt: (b, 0, 0)),
                 pl.BlockSpec(wukt.shape, lambda b, s, pt: (0, 0)),
                 pl.BlockSpec((2 * rope, n_group * page), lambda b, s, pt: (0, s)),
                 pl.BlockSpec((2 * rope, page), lambda b, s, pt: (0, 0))]
    grid_spec = pltpu.PrefetchScalarGridSpec(
        num_scalar_prefetch=1,
        grid=(nb, nsteps),
        in_specs=in_specs,
        out_specs=pl.BlockSpec((None, kv_lora, nrow), lambda b, s, pt: (b, 0, 0)),
        scratch_shapes=[pltpu.VMEM((wukt.shape[0] + nrow, kv_lora), BF16),
                        pltpu.VMEM((kv_lora, n_group * page), BF16),
                        pltpu.VMEM((nrow, 1), F32), pltpu.VMEM((nrow, 1), F32),
                        pltpu.VMEM((kv_lora, nrow), F32)])
    return pl.pallas_call(
        kern,
        grid_spec=grid_spec,
        out_shape=jax.ShapeDtypeStruct((nb, kv_lora, nrow), F32),
        compiler_params=_cparams("parallel", "arbitrary"),
        name="sample_attention",
    )(page_table.reshape(-1), *([cache_t] * n_group), new_pages_t, qf, wukt, cs_pages, cs_new)


def _uv_kernel(lm_ref, w_ref, o_ref):
    o_ref[...] = _dot(lm_ref[...].astype(BF16), w_ref[...])


def _uv_project(lm, wuv):
    t = lm.shape[0]
    nheads, c, dv = wuv.shape
    return pl.pallas_call(
        _uv_kernel,
        grid=(nheads,),
        in_specs=[pl.BlockSpec((t, c), lambda h: (0, h)),
                  pl.BlockSpec((None, c, dv), lambda h: (h, 0, 0))],
        out_specs=pl.BlockSpec((t, dv), lambda h: (0, h)),
        out_shape=jax.ShapeDtypeStruct((t, nheads * dv), F32),
        compiler_params=_cparams("parallel"),
        name="value_up_projection",
    )(lm, wuv)


def _cumsum_rows(g, chunk):
    if chunk <= SUBLANES:
        ridx = lax.broadcasted_iota(jnp.int32, g.shape, 0)
        out = jnp.zeros_like(g)
        for s in range(chunk):
            out = out + jnp.where(ridx >= s, g[s:s + 1, :], 0.0)
        return out
    rows = lax.broadcasted_iota(jnp.int32, (chunk, chunk), 0)
    cols = lax.broadcasted_iota(jnp.int32, (chunk, chunk), 1)
    tri = (cols <= rows).astype(BF16)
    hi = g.astype(BF16)
    r1 = g - hi.astype(F32)
    mid = r1.astype(BF16)
    lo = (r1 - mid.astype(F32)).astype(BF16)
    return _dot(tri, hi) + (_dot(tri, mid) + _dot(tri, lo))


def _hgrn_kernel(hq_ref, hf_ref, hi_ref, hg_ref, lbp_ref, s0_ref, gnw_ref, o_ref, sout_ref, st_scr, *,
                 chunk, sub, nchunk, t_valid, has_s0, hps):
    tb = pl.program_id(2)
    last = tb == pl.num_programs(2) - 1
    gnw = gnw_ref[...]
    for hh in range(hps):
        cs = slice(hh * LANES, (hh + 1) * LANES)

        @pl.when(tb == 0)
        def _():
            if has_s0:
                st_scr[hh] = s0_ref[hh].T
            else:
                st_scr[hh] = jnp.zeros(st_scr.shape[1:], F32)

        a = lbp_ref[:, cs]
        e = jnp.exp(a - jnp.max(a, axis=0, keepdims=True))
        lb = e[0:1, :] / jnp.sum(e, axis=0, keepdims=True)

        st = st_scr[hh]
        for c in range(nchunk):
            sl = slice(c * chunk, (c + 1) * chunk)
            zq = hq_ref[sl, cs]
            q = zq * _sigmoid(zq)
            k = (1.0 - lb) * _sigmoid(-hf_ref[sl, cs])
            g = jnp.log(1.0 - k)
            if t_valid is not None:
                valid = (tb * (nchunk * chunk) + c * chunk
                         + lax.broadcasted_iota(jnp.int32, k.shape, 0)) < t_valid
                k = jnp.where(valid, k, 0.0)
                g = jnp.where(valid, g, 0.0)
            v = hi_ref[sl, cs]
            zg = hg_ref[sl, cs]
            gate = zg * _sigmoid(zg)

            gc = _cumsum_rows(g, chunk)
            o = _dot_nt(q * jnp.exp(gc), st)
            g_last = gc[chunk - 1:chunk, :]
            u_t = _dot_tn(v, k * jnp.exp(g_last - gc))

            parts = []
            for i in range(chunk // sub):
                r0 = i * sub
                gi = gc[r0:r0 + sub, :]
                qi = q[r0:r0 + sub, :]
                if i > 0:
                    ref_row = gc[r0 - 1:r0, :]
                    a_off = _dot_nt(qi * jnp.exp(gi - ref_row), k[:r0, :] * jnp.exp(ref_row - gc[:r0, :]))
                    o_i = _dot(a_off, v[:r0, :])
                else:
                    o_i = jnp.zeros((sub, v.shape[1]), F32)
                trow = lax.broadcasted_iota(jnp.int32, (sub, 1), 0)
                for s in range(sub):
                    keep = trow >= s
                    d = jnp.where(keep, gi - gc[r0 + s:r0 + s + 1, :], 0.0)
                    w = jnp.exp(d) * qi * k[r0 + s:r0 + s + 1, :]
                    a_col = jnp.where(keep, jnp.sum(w, axis=-1, keepdims=True), 0.0)
                    o_i = o_i + a_col * v[r0 + s:r0 + s + 1, :]
                parts.append(o_i)
            o = o + (parts[0] if len(parts) == 1 else jnp.concatenate(parts, axis=0))
            st = st * jnp.exp(g_last) + u_t
            o_ref[sl, cs] = _rms(o) * gnw * gate

        st_scr[hh] = st

        @pl.when(last)
        def _():
            sout_ref[hh] = st.T


def _hgrn(zh, lbp, s0, gnw, *, nheads, hps, tblock, chunk, sub, t_valid):
    n, t, _ = zh.shape
    dk = LANES
    has_s0 = s0 is not None
    nlb = lbp.shape[0]
    ngrp = nheads // hps
    if not has_s0:
        s0 = jnp.zeros((1, hps, dk, dk), F32)
        s0_spec = pl.BlockSpec((None, hps, dk, dk), lambda b, h, tb: (0, 0, 0, 0))
    else:
        s0_spec = pl.BlockSpec((None, hps, dk, dk), lambda b, h, tb: (b, h, 0, 0))

    def col(group):
        return pl.BlockSpec((None, tblock, hps * dk), lambda b, h, tb: (b, tb, group * ngrp + h))

    kern = functools.partial(_hgrn_kernel, chunk=chunk, sub=sub, nchunk=tblock // chunk, t_valid=t_valid,
                             has_s0=has_s0, hps=hps)
    return pl.pallas_call(
        kern,
        grid=(n, ngrp, t // tblock),
        in_specs=[col(0), col(1), col(2), col(3),
                  pl.BlockSpec((nlb, hps * dk), lambda b, h, tb: (0, h)),
                  s0_spec,
                  pl.BlockSpec((1, dk), lambda b, h, tb: (0, 0))],
        out_specs=[pl.BlockSpec((None, tblock, hps * dk), lambda b, h, tb: (b, tb, h)),
                   pl.BlockSpec((None, hps, dk, dk), lambda b, h, tb: (b, h, 0, 0))],
        out_shape=[jax.ShapeDtypeStruct((n, t, nheads * dk), F32),
                   jax.ShapeDtypeStruct((n, nheads, dk, dk), F32)],
        scratch_shapes=[pltpu.VMEM((hps, dk, dk), F32)],
        compiler_params=_cparams("parallel", "parallel", "arbitrary"),
        name="hgrn2",
    )(zh, zh, zh, zh, lbp, s0, gnw)


def _outproj_kernel(oa_ref, orec_ref, x_ref, ga_ref, scf_ref, shf_ref, gnw_ref, fnw_ref, wo_ref, rw_ref,
                    rb_ref, x1_ref, h2_ref, topi_ref, gate_ref, *, n_attn):
    oan = (_rms(oa_ref[...]) * gnw_ref[...]).astype(BF16)
    mix = _dot(oan, wo_ref[:n_attn, :]) + _dot(orec_ref[...].astype(BF16), wo_ref[n_attn:, :])
    x1 = x_ref[...] + ga_ref[...] * mix
    x1_ref[...] = x1
    h2 = _rms(x1) * fnw_ref[...] * (1.0 + scf_ref[...]) + shf_ref[...]
    h2_ref[...] = h2
    logits = jnp.dot(h2, rw_ref[...], preferred_element_type=F32,
                     precision=lax.Precision.HIGHEST) + rb_ref[...]
    lane = lax.broadcasted_iota(jnp.int32, logits.shape, 1).astype(F32)
    vals = []
    topi = jnp.zeros(logits.shape, F32)
    work = logits
    for kk in range(TOP_K):
        m = jnp.max(work, axis=-1, keepdims=True)
        idx = jnp.min(jnp.where(work == m, lane, float(LANES)), axis=-1, keepdims=True)
        vals.append(m)
        topi = jnp.where(lane == float(kk), idx, topi)
        work = jnp.where(lane == idx, -3e38, work)
    es = [jnp.exp(vv - vals[0]) for vv in vals]
    den = es[0]
    for ee in es[1:]:
        den = den + ee
    gates = jnp.zeros(logits.shape, F32)
    for kk in range(TOP_K):
        gates = jnp.where(lane == float(kk), es[kk] / den, gates)
    topi_ref[...] = topi.astype(jnp.int32)
    gate_ref[...] = gates


def _outproj(oa, orec, x, ga, scf, shf, mod_spec, gnw, fnw, wo, rw, rb, *, tm):
    t, d = x.shape
    n_attn = oa.shape[1]
    n_rec = orec.shape[1]
    row = lambda i: (i, 0)
    const = lambda i: (0, 0)
    return pl.pallas_call(
        functools.partial(_outproj_kernel, n_attn=n_attn),
        grid=(t // tm,),
        in_specs=[pl.BlockSpec((tm, n_attn), row), pl.BlockSpec((tm, n_rec), row), pl.BlockSpec((tm, d), row),
                  mod_spec, mod_spec, mod_spec,
                  pl.BlockSpec((1, n_attn), const), pl.BlockSpec((1, d), const),
                  pl.BlockSpec(wo.shape, const), pl.BlockSpec(rw.shape, const), pl.BlockSpec((1, LANES), const)],
        out_specs=[pl.BlockSpec((tm, d), row), pl.BlockSpec((tm, d), row),
                   pl.BlockSpec((tm, LANES), row), pl.BlockSpec((tm, LANES), row)],
        out_shape=[jax.ShapeDtypeStruct((t, d), F32), jax.ShapeDtypeStruct((t, d), F32),
                   jax.ShapeDtypeStruct((t, LANES), jnp.int32), jax.ShapeDtypeStruct((t, LANES), F32)],
        compiler_params=_cparams("parallel"),
        name="outproj_router",
    )(oa, orec, x, ga, scf, shf, gnw, fnw, wo, rw, rb)


def _gather_kernel(idx_ref, src_ref, dst_ref, sem, *, rows_per_step):
    def row_copy(r, src_row):
        return pltpu.make_async_copy(src_ref.at[src_row], dst_ref.at[r], sem)

    def issue(r, carry):
        row_copy(r, idx_ref[r]).start()
        return carry

    def drain(r, carry):
        row_copy(r, 0).wait()
        return carry

    lax.fori_loop(0, rows_per_step, issue, 0)
    lax.fori_loop(0, rows_per_step, drain, 0)


def _gather_rows(idx, src, *, rows_per_step):
    n = idx.shape[0]
    return pl.pallas_call(
        functools.partial(_gather_kernel, rows_per_step=rows_per_step),
        grid=(n // rows_per_step,),
        in_specs=[pl.BlockSpec((rows_per_step,), lambda i: (i,), memory_space=pltpu.SMEM),
                  pl.BlockSpec(memory_space=pl.ANY)],
        out_specs=pl.BlockSpec((rows_per_step,) + src.shape[1:], lambda i: (i, 0, 0)),
        out_shape=jax.ShapeDtypeStruct((n,) + src.shape[1:], src.dtype),
        scratch_shapes=[pltpu.SemaphoreType.DMA(())],
        compiler_params=_cparams("parallel"),
        name="gather_rows",
    )(idx, src)


def _moe_kernel(te_ref, nu_ref, idx0_ref, idxn_ref, h_ref, w1g_ref, w1l_ref, b1g_ref, b1l_ref, w2_ref, b2_ref,
                o_ref, xbuf, xb_scr, acc_scr, sem, *, nseg, nj):
    i = pl.program_id(0)
    j = pl.program_id(1)
    tm = xb_scr.shape[0]
    npk = nseg // 2
    half = npk * LANES
    per = tm // nj
    nu = nu_ref[0]

    def row_copy(slot, r, src_row):
        dst = xbuf.at[pl.ds(pl.multiple_of((slot * tm + r) * npk, npk), npk), :]
        return pltpu.make_async_copy(h_ref.at[src_row], dst, sem.at[slot])

    def issue_rows(slot, idx_ref, start, count):
        def body(r, carry):
            row_copy(slot, r, idx_ref[r]).start()
            return carry
        lax.fori_loop(start, start + count, body, 0)

    def wait_rows(slot):
        def body(r, carry):
            row_copy(slot, r, 0).wait()
            return carry
        lax.fori_loop(0, tm, body, 0)

    @pl.when(jnp.logical_and(i == 0, j == 0))
    def _():
        issue_rows(0, idx0_ref, 0, tm)

    for slot in (0, 1):
        @pl.when(jnp.logical_and(i % 2 == slot, i < nu))
        def _():
            @pl.when(i + 1 < nu)
            def _():
                issue_rows(1 - slot, idxn_ref, j * per, per)

            @pl.when(j == 0)
            def _():
                wait_rows(slot)
                for s in range(npk):
                    u = xbuf[pl.ds(slot * tm * npk + s, tm, stride=npk), :]
                    lo = lax.bitcast_convert_type(u << 16, F32)
                    hi = lax.bitcast_convert_type(u & jnp.uint32(0xFFFF0000), F32)
                    xb_scr[:, s * LANES:(s + 1) * LANES] = lo.astype(BF16)
                    xb_scr[:, half + s * LANES:half + (s + 1) * LANES] = hi.astype(BF16)
                acc_scr[...] = jnp.broadcast_to(b2_ref[...], acc_scr.shape)

    @pl.when(i < nu)
    def _():
        xb = xb_scr[...]
        yg = _dot(xb, w1g_ref[...].astype(BF16)) + b1g_ref[...]
        yl = _dot(xb, w1l_ref[...].astype(BF16)) + b1l_ref[...]
        glu = jnp.minimum(yg, SWIGLU_LIMIT)
        lin = jnp.clip(yl, -SWIGLU_LIMIT, SWIGLU_LIMIT)
        a = (glu * _sigmoid(SWIGLU_ALPHA * glu) * (lin + 1.0)).astype(BF16)
        acc_scr[...] += _dot(a, w2_ref[...].astype(BF16))

        @pl.when(j == nj - 1)
        def _():
            for s in range(nseg):
                o_ref[pl.ds(s, tm, stride=nseg), :] = acc_scr[:, s * LANES:(s + 1) * LANES]

    @pl.when(jnp.logical_and(i >= nu, j == 0))
    def _():
        o_ref[...] = jnp.zeros(o_ref.shape, F32)


def _moe_ffn(tile_e, n_used, slot_tok, h2p, w1, b1, w2, b2, *, tm, th, nseg):
    d = nseg * LANES
    p = slot_tok.shape[0]
    ne, _, two_de = w1.shape
    de = two_de // 2
    nj = de // th
    ntiles = p // tm
    assert tm % nj == 0 and nseg % 2 == 0

    def tile(i, nu):
        return jnp.minimum(i, nu[0] - 1)

    def hid(i, j, nu):
        return jnp.where(i < nu[0], j, nj - 1)

    grid_spec = pltpu.PrefetchScalarGridSpec(
        num_scalar_prefetch=2,
        grid=(ntiles, nj),
        in_specs=[pl.BlockSpec((tm,), lambda i, j, te, nu: (0,), memory_space=pltpu.SMEM),
                  pl.BlockSpec((tm,), lambda i, j, te, nu: (jnp.minimum(i + 1, ntiles - 1),),
                               memory_space=pltpu.SMEM),
                  pl.BlockSpec(memory_space=pl.ANY),
                  pl.BlockSpec((None, d, th), lambda i, j, te, nu: (te[tile(i, nu)], 0, hid(i, j, nu))),
                  pl.BlockSpec((None, d, th), lambda i, j, te, nu: (te[tile(i, nu)], 0, nj + hid(i, j, nu))),
                  pl.BlockSpec((None, 1, th), lambda i, j, te, nu: (te[tile(i, nu)], 0, hid(i, j, nu))),
                  pl.BlockSpec((None, 1, th), lambda i, j, te, nu: (te[tile(i, nu)], 0, nj + hid(i, j, nu))),
                  pl.BlockSpec((None, th, d), lambda i, j, te, nu: (te[tile(i, nu)], hid(i, j, nu), 0)),
                  pl.BlockSpec((None, 1, d), lambda i, j, te, nu: (te[tile(i, nu)], 0, 0))],
        out_specs=pl.BlockSpec((tm * nseg, LANES), lambda i, j, te, nu: (i, 0)),
        scratch_shapes=[pltpu.VMEM((2 * tm * (nseg // 2), LANES), jnp.uint32),
                        pltpu.VMEM((tm, d), BF16),
                        pltpu.VMEM((tm, d), F32),
                        pltpu.SemaphoreType.DMA((2,))])
    return pl.pallas_call(
        functools.partial(_moe_kernel, nseg=nseg, nj=nj),
        grid_spec=grid_spec,
        out_shape=jax.ShapeDtypeStruct((p * nseg, LANES), F32),
        compiler_params=_cparams("arbitrary", "arbitrary"),
        name="moe_experts",
    )(tile_e, n_used, slot_tok, slot_tok, h2p, w1, w1, b1.reshape(ne, 1, two_de),
      b1.reshape(ne, 1, two_de), w2, b2.reshape(ne, 1, d))


def _combine_kernel(*refs, nseg):
    y_refs = refs[:TOP_K]
    gate_ref, x1_ref, gf_ref, o_ref = refs[TOP_K:]
    tm = x1_ref.shape[0]
    gates = gate_ref[...]
    gk = [gates[:, kk:kk + 1] for kk in range(TOP_K)]
    for s in range(nseg):
        seg = slice(s * LANES, (s + 1) * LANES)
        acc = gk[0] * y_refs[0][pl.ds(s, tm, stride=nseg), :]
        for kk in range(1, TOP_K):
            acc = acc + gk[kk] * y_refs[kk][pl.ds(s, tm, stride=nseg), :]
        o_ref[:, seg] = x1_ref[:, seg] + gf_ref[:, seg] * acc


def _combine(yg2, gates, x1, gf, mod_spec, *, tm, row_off, nblk_all, nseg):
    t, d = x1.shape

    def ymap(kk):
        return lambda i: (kk * nblk_all + row_off + i, 0)

    return pl.pallas_call(
        functools.partial(_combine_kernel, nseg=nseg),
        grid=(t // tm,),
        in_specs=[pl.BlockSpec((tm * nseg, LANES), ymap(kk)) for kk in range(TOP_K)] + [
            pl.BlockSpec((tm, LANES), lambda i: (i + row_off, 0)),
            pl.BlockSpec((tm, d), lambda i: (i, 0)),
            mod_spec],
        out_specs=pl.BlockSpec((tm, d), lambda i: (i, 0)),
        out_shape=jax.ShapeDtypeStruct((t, d), F32),
        compiler_params=_cparams("parallel"),
        name="moe_combine",
    )(*([yg2] * TOP_K), gates, x1, gf)


def _rope_angles(pos, half):
    inv = ROPE_THETA ** (-jnp.arange(half, dtype=F32) / half)
    ang = pos.astype(F32)[:, None] * inv[None, :]
    return jnp.cos(ang), jnp.sin(ang)


def _pad_lanes(v, width=LANES):
    v = v.reshape(1, -1).astype(F32)
    return jnp.pad(v, ((0, 0), (0, width - v.shape[1])))


def _swap_halves(v):
    h = v.shape[-1] // 2
    return jnp.concatenate([v[..., h:], v[..., :h]], -1)


def kernel(x_prompt, x_sample, cache_mla, state_hgrn, page_table, c_prompt, c_sample, w_ada, b_ada,
           attn_norm_w, w_in, q_a_norm_w, w_uq, kv_a_norm_w, w_ukv, q_head_norm_w, k_head_norm_w,
           attn_group_norm_w, hg_lower_bound, hg_norm_w, w_out, ffn_norm_w, router_w, router_b,
           w1, b1, w2, b2):
    nb_p, seq, d = x_prompt.shape
    nb_s, ntok, _ = x_sample.shape
    depth = w_ada.shape[0]
    assert depth == 1, "single-layer step"
    page = cache_mla.shape[2]
    cache_w = cache_mla.shape[3]
    n_pages = page_table.shape[1]
    past_len = n_pages * page
    q_lora, nheads, d_head = w_uq.shape[1], w_uq.shape[2], w_uq.shape[3]
    kv_lora = w_ukv.shape[1]
    rope = cache_w - kv_lora
    nope = d_head - rope
    v_head = w_ukv.shape[3] - nope
    hg_heads, hg_dk, hg_dv = state_hgrn.shape[2], state_hgrn.shape[3], state_hgrn.shape[4]
    n_experts = router_w.shape[2]
    assert nope == LANES and v_head == LANES and hg_dk == LANES and hg_dv == LANES
    assert 2 * rope == LANES and n_experts <= LANES and d % LANES == 0
    half = rope // 2
    nseg = d // LANES
    attn_scale = d_head ** -0.5
    t_p = nb_p * seq
    t_s = nb_s * ntok

    mod = _adaln(jnp.concatenate([c_prompt, c_sample], 0), w_ada[0], b_ada[0])
    mods = [mod[:, i * d:(i + 1) * d] for i in range(6)]
    mods_p = [m[:nb_p].reshape(nb_p, 1, d) for m in mods]
    mods_s = [jnp.repeat(m[nb_p:], ntok, axis=0) for m in mods]

    wi = w_in[0]
    o_kr = q_lora + kv_lora
    w_kr = wi[:, o_kr:o_kr + rope]
    zpad = jnp.zeros((d, LANES - rope), F32)
    wa = jnp.concatenate([wi[:, :o_kr], w_kr, zpad, _swap_halves(w_kr), zpad], 1).astype(BF16)
    wh = wi[:, o_kr + rope:].astype(BF16)
    wq_full = jnp.transpose(w_uq[0], (1, 0, 2))
    wq_r = wq_full[:, :, nope:]
    wq_rs = _swap_halves(wq_r)
    wq = jnp.concatenate([wq_full[:, :, :nope], wq_r, wq_rs, wq_rs, wq_r], -1).astype(BF16)
    wkv = jnp.transpose(w_ukv[0], (1, 0, 2)).astype(BF16)
    wuk = wkv[:, :, :nope]
    wuv = wkv[:, :, nope:]
    wukt = jnp.transpose(wuk, (0, 2, 1)).reshape(nheads * nope, kv_lora)
    qn_w, kn_w = q_head_norm_w[0], k_head_norm_w[0]
    qr_w, kr_w = qn_w[nope:], kn_w[nope:]
    qw_n = _pad_lanes(qn_w[:nope])
    qw_a = jnp.concatenate([qr_w, _swap_halves(qr_w)]).reshape(1, LANES)
    qw_b = jnp.concatenate([_swap_halves(qr_w), qr_w]).reshape(1, LANES)
    kw_n, kw_a, kw_b = _pad_lanes(kn_w[:nope]), _pad_lanes(kr_w), _pad_lanes(_swap_halves(kr_w))
    kw_rr = jnp.concatenate([kr_w, kr_w]).reshape(1, LANES)
    nw = attn_norm_w[0].reshape(1, d)
    qaw = q_a_norm_w[0].reshape(1, q_lora)
    kvw = kv_a_norm_w[0].reshape(1, kv_lora)

    def token_tables(pos):
        cos, sin = _rope_angles(pos, half)
        z = jnp.zeros((pos.shape[0], LANES - rope), F32)
        k_cos = jnp.concatenate([cos, cos, z], -1)
        k_sin = jnp.concatenate([-sin, sin, z], -1)
        q_cos = jnp.concatenate([cos, cos, cos, -cos], -1)
        q_sin = jnp.concatenate([-sin, sin, sin, sin], -1)
        return k_cos, k_sin, q_cos, q_sin

    tabs_p = token_tables(jnp.arange(seq))
    tabs_s = token_tables(past_len + (jnp.arange(t_s) % ntok))

    tm_p = min(512, seq)
    tm_s = t_s
    nblk = seq // tm_p
    modspec_p2 = pl.BlockSpec((None, 1, d), lambda i, j: (i // nblk, 0, 0))
    modspec_s2 = pl.BlockSpec((tm_s, d), lambda i, j: (i, 0))
    dims = dict(q_lora=q_lora, kv_lora=kv_lora, rope=rope)

    xp = x_prompt.reshape(t_p, d)
    xs = x_sample.reshape(t_s, d)
    cqn_p, rows_p, latb_p, krr_p, krsq_p, zh_p = _inproj(
        xp, mods_p[1], mods_p[0], modspec_p2, nw, wa, wh, qaw, kvw, kw_a, kw_b, tabs_p[0], tabs_p[1],
        lambda i, j: (i % nblk, 0), tm=tm_p, **dims)
    cqn_s, rows_s, latb_s, krr_s, krsq_s, zh_s = _inproj(
        xs, mods_s[1], mods_s[0], modspec_s2, nw, wa, wh, qaw, kvw, kw_a, kw_b, tabs_s[0], tabs_s[1],
        lambda i, j: (i, 0), tm=tm_s, **dims)

    q_p = _queries(cqn_p, wq, qw_n, qw_a, qw_b, tabs_p[2:], lambda i, h: (i % nblk, 0), tm=tm_p,
                   scale=attn_scale, d_head=d_head)
    k_p, v_p = _keys_values(latb_p, krr_p, krsq_p, wkv, kw_n, tm=tm_p, d_head=d_head)
    oa_p = _flash(q_p, k_p, v_p, nbatch=nb_p, seq=seq, tq=tm_p)

    q_s = _queries(cqn_s, wq, qw_n, qw_a, qw_b, tabs_s[2:], lambda i, h: (i, 0), tm=tm_s,
                   scale=attn_scale, d_head=d_head)
    qf = _absorb(q_s, wuk, kw_n, kw_rr, kv_lora=kv_lora)
    qf = qf.reshape(nb_s, ntok * nheads, kv_lora + LANES)

    def cs_table(pos):
        cos, sin = _rope_angles(pos, half)
        return jnp.concatenate([cos, cos, sin, sin], -1).T

    cs_pages = cs_table(jnp.arange(past_len))
    cs_new = cs_table(past_len + jnp.arange(page))
    cache_t = jnp.swapaxes(cache_mla, 2, 3)
    new_pages = jnp.zeros((nb_s, page, cache_w), F32).at[:, :ntok].set(rows_s.reshape(nb_s, ntok, cache_w))
    new_pages_t = jnp.swapaxes(new_pages, 1, 2)
    n_group = 8
    while n_pages % n_group:
        n_group //= 2
    lat_mix = _sample_attention(page_table, cache_t, new_pages_t, qf, wukt, cs_pages, cs_new,
                                n_group=n_group, kv_lora=kv_lora, rope=rope, nheads=nheads, ntok=ntok,
                                d_head=d_head)
    lat_mix = jnp.swapaxes(lat_mix, 1, 2)
    oa_s = _uv_project(lat_mix.reshape(t_s, nheads * kv_lora), wuv)

    gnw_h = hg_norm_w[0].reshape(1, hg_dv)
    chunk_p = min(HG_CHUNK, seq)
    orec_p, st_p = _hgrn(zh_p.reshape(nb_p, seq, -1), hg_lower_bound, None, gnw_h, nheads=hg_heads, hps=1,
                         tblock=min(256, seq), chunk=chunk_p, sub=min(HG_SUB, chunk_p), t_valid=None)
    t_pad = -(-ntok // SUBLANES) * SUBLANES
    zh_s3 = jnp.pad(zh_s.reshape(nb_s, ntok, -1), ((0, 0), (0, t_pad - ntok), (0, 0)))
    orec_s, st_s = _hgrn(zh_s3, hg_lower_bound, state_hgrn[0], gnw_h, nheads=hg_heads, hps=hg_heads,
                         tblock=t_pad, chunk=t_pad, sub=t_pad, t_valid=ntok)
    orec_s = orec_s[:, :ntok].reshape(t_s, -1)

    wo = w_out[0].astype(BF16)
    rw = jnp.pad(router_w[0], ((0, 0), (0, LANES - n_experts)))
    rb = jnp.concatenate([router_b[0], jnp.full((LANES - n_experts,), NEG_INF, F32)]).reshape(1, LANES)
    gnw_a = attn_group_norm_w[0].reshape(1, -1)
    fnw = ffn_norm_w[0].reshape(1, d)
    modspec_p1 = pl.BlockSpec((None, 1, d), lambda i: (i // nblk, 0, 0))
    modspec_s1 = pl.BlockSpec((tm_s, d), lambda i: (i, 0))
    x1_p, h2_p, ti_p, gt_p = _outproj(oa_p, orec_p.reshape(t_p, -1), xp, mods_p[2], mods_p[4], mods_p[3],
                                      modspec_p1, gnw_a, fnw, wo, rw, rb, tm=tm_p)
    x1_s, h2_s, ti_s, gt_s = _outproj(oa_s, orec_s, xs, mods_s[2], mods_s[4], mods_s[3],
                                      modspec_s1, gnw_a, fnw, wo, rw, rb, tm=tm_s)

    n_tok = t_p + t_s
    h2b = jnp.concatenate([h2_p, h2_s], 0).astype(BF16)
    h2w = lax.bitcast_convert_type(h2b, jnp.uint16).astype(jnp.uint32)
    h2p = (h2w[:, :d // 2] | (h2w[:, d // 2:] << 16)).reshape(n_tok, nseg // 2, LANES)
    top_i = jnp.concatenate([ti_p[:, :TOP_K], ti_s[:, :TOP_K]], 0)
    gates = jnp.concatenate([gt_p, gt_s], 0)
    rows_per_step = 1024
    tm_e = rows_per_step if n_tok >= rows_per_step else 1 << (n_tok.bit_length() - 1)
    n_assign = n_tok * TOP_K
    flat_e = top_i.reshape(-1)
    onehot = (flat_e[:, None] == jnp.arange(n_experts, dtype=jnp.int32)[None, :]).astype(jnp.int32)
    csum = jnp.cumsum(onehot, axis=0)
    counts = csum[-1]
    rank = jnp.take_along_axis(csum, flat_e[:, None], axis=1)[:, 0] - 1
    padded = (counts + tm_e - 1) // tm_e * tm_e
    pend = jnp.cumsum(padded)
    pstart = pend - padded
    dest = (pstart[flat_e] + rank).astype(jnp.int32)
    n_slots = -(-(n_assign + n_experts * tm_e) // rows_per_step) * rows_per_step
    n_slots = -(-n_slots // tm_e) * tm_e
    flat_tok = jnp.arange(n_assign, dtype=jnp.int32) // TOP_K
    slot_tok = jnp.zeros((n_slots,), jnp.int32).at[dest].set(flat_tok)
    n_tiles = n_slots // tm_e
    tile_e = jnp.minimum(jnp.searchsorted(pend, jnp.arange(n_tiles, dtype=jnp.int32) * tm_e, side='right'),
                         n_experts - 1).astype(jnp.int32)
    n_used = (pend[-1] // tm_e).astype(jnp.int32).reshape(1)

    ys2 = _moe_ffn(tile_e, n_used, slot_tok, h2p, w1[0], b1[0], w2[0], b2[0],
                   tm=tm_e, th=min(256, d), nseg=nseg)
    tm_c = min(256, t_s)
    assert n_tok % tm_c == 0 and t_p % tm_c == 0
    dest_kmajor = dest.reshape(n_tok, TOP_K).T.reshape(-1)
    n_pad = -(-n_assign // rows_per_step) * rows_per_step
    dest_kmajor = jnp.pad(dest_kmajor, (0, n_pad - n_assign))
    yg = _gather_rows(dest_kmajor, ys2.reshape(n_slots, nseg, LANES), rows_per_step=rows_per_step)
    yg2 = yg.reshape(n_pad * nseg, LANES)

    nblk_all = n_tok // tm_c
    y_p = _combine(yg2, gates, x1_p, mods_p[5], pl.BlockSpec((None, 1, d), lambda i: (i // (seq // tm_c), 0, 0)),
                   tm=tm_c, row_off=0, nblk_all=nblk_all, nseg=nseg)
    y_s = _combine(yg2, gates, x1_s, mods_s[5], pl.BlockSpec((tm_c, d), lambda i: (i, 0)),
                   tm=tm_c, row_off=t_p // tm_c, nblk_all=nblk_all, nseg=nseg)

    return (y_p.reshape(nb_p, seq, d), y_s.reshape(nb_s, ntok, d),
            rows_p.reshape(1, nb_p, seq, cache_w), rows_s.reshape(1, nb_s, ntok, cache_w),
            st_p[None], st_s[None])
```

```python
import functools

import jax
import jax.numpy as jnp
from jax import lax
from jax.experimental import pallas as pl
from jax.experimental.pallas import tpu as pltpu

F32 = jnp.float32
BF16 = jnp.bfloat16

NORM_EPS = 1e-6
NEG_INF = -1e30
ROPE_THETA = 10000.0
TOP_K = 4
SWIGLU_LIMIT = 7.0
SWIGLU_ALPHA = 1.702
HG_CHUNK = 64
HG_SUB = 16
LANES = 128
SUBLANES = 8
MXU_WIDTH = 256
VMEM_LIMIT_BYTES = 56 * 1024 * 1024


def _cparams(*sem):
    return pltpu.CompilerParams(dimension_semantics=sem, vmem_limit_bytes=VMEM_LIMIT_BYTES)


def _pick(n, pref, mult=LANES):
    if n <= pref:
        return n
    t = pref - pref % mult
    while t > mult and n % t:
        t -= mult
    assert n % t == 0, (n, pref, mult)
    return t


def _sigmoid(x):
    return 1.0 / (1.0 + jnp.exp(-x))


def _rms(x, eps=NORM_EPS):
    return x * lax.rsqrt(jnp.mean(x * x, axis=-1, keepdims=True) + eps)


def _dot(a, b):
    return jnp.dot(a, b, preferred_element_type=F32)


def _dot_nt(a, b):
    return lax.dot_general(a, b, (((1,), (1,)), ((), ())), preferred_element_type=F32)


def _dot_tn(a, b):
    return lax.dot_general(a, b, (((0,), (0,)), ((), ())), preferred_element_type=F32)


def _adaln_kernel(c_ref, w_ref, b_ref, o_ref):
    c = c_ref[...]
    a = (c * _sigmoid(c)).astype(BF16)
    o_ref[...] = _dot(a, w_ref[...].astype(BF16)) + b_ref[...]


def _adaln(c, w, b):
    n, d = c.shape
    nout = w.shape[1]
    tn = _pick(nout, 1024)
    return pl.pallas_call(
        _adaln_kernel,
        grid=(nout // tn,),
        in_specs=[pl.BlockSpec((n, d), lambda j: (0, 0)),
                  pl.BlockSpec((d, tn), lambda j: (0, j)),
                  pl.BlockSpec((1, tn), lambda j: (0, j))],
        out_specs=pl.BlockSpec((n, tn), lambda j: (0, j)),
        out_shape=jax.ShapeDtypeStruct((n, nout), F32),
        compiler_params=_cparams("parallel"),
        name="adaln",
    )(c, w, b.reshape(1, nout))


def _inproj_kernel(x_ref, sc_ref, sh_ref, nw_ref, wa_ref, wh_ref, qaw_ref, kvw_ref, wkr_ref, wkrs_ref,
                   cos_ref, sin_ref,
                   cqn_ref, rows_ref, latb_ref, krr_ref, krsq_ref, zh_ref, h_scr, *, q_lora, kv_lora, rope):
    j = pl.program_id(1)

    @pl.when(j == 0)
    def _():
        h = _rms(x_ref[...]) * nw_ref[...]
        h = h * (1.0 + sc_ref[...]) + sh_ref[...]
        hb = h.astype(BF16)
        h_scr[...] = hb
        za = _dot(hb, wa_ref[...])
        cqn_ref[...] = (_rms(za[:, :q_lora]) * qaw_ref[...]).astype(BF16)
        lat = _rms(za[:, q_lora:q_lora + kv_lora]) * kvw_ref[...]
        o = q_lora + kv_lora
        kr_a = za[:, o:o + LANES]
        kr_b = za[:, o + LANES:o + 2 * LANES]
        rows_ref[:, :kv_lora] = lat
        rows_ref[:, kv_lora:] = kr_a[:, :rope]
        latb_ref[...] = lat.astype(BF16)
        krr_ref[...] = kr_a * wkr_ref[...] * cos_ref[...] + kr_b * wkrs_ref[...] * sin_ref[...]
        krsq_ref[...] = jnp.broadcast_to(jnp.sum(kr_a * kr_a, axis=-1, keepdims=True), krsq_ref.shape)

    @pl.when(j > 0)
    def _():
        zh_ref[...] = _dot(h_scr[...], wh_ref[...])


def _inproj(x, sc, sh, mod_spec, nw, wa, wh, qaw, kvw, wkr, wkrs, cos_t, sin_t, tab_map, *, tm,
            q_lora, kv_lora, rope):
    t, d = x.shape
    na = wa.shape[1]
    nh = wh.shape[1]
    tn = _pick(nh, 1024)
    nj = nh // tn
    const = lambda i, j: (0, 0)
    row = lambda i, j: (i, 0)
    kern = functools.partial(_inproj_kernel, q_lora=q_lora, kv_lora=kv_lora, rope=rope)
    return pl.pallas_call(
        kern,
        grid=(t // tm, 1 + nj),
        in_specs=[pl.BlockSpec((tm, d), row), mod_spec, mod_spec,
                  pl.BlockSpec((1, d), const),
                  pl.BlockSpec((d, na), const),
                  pl.BlockSpec((d, tn), lambda i, j: (0, jnp.maximum(j - 1, 0))),
                  pl.BlockSpec((1, q_lora), const), pl.BlockSpec((1, kv_lora), const),
                  pl.BlockSpec((1, LANES), const), pl.BlockSpec((1, LANES), const),
                  pl.BlockSpec((tm, LANES), tab_map), pl.BlockSpec((tm, LANES), tab_map)],
        out_specs=[pl.BlockSpec((tm, q_lora), row),
                   pl.BlockSpec((tm, kv_lora + rope), row),
                   pl.BlockSpec((tm, kv_lora), row),
                   pl.BlockSpec((tm, LANES), row),
                   pl.BlockSpec((tm, LANES), row),
                   pl.BlockSpec((tm, tn), lambda i, j: (i, jnp.maximum(j - 1, 0)))],
        out_shape=[jax.ShapeDtypeStruct((t, q_lora), BF16),
                   jax.ShapeDtypeStruct((t, kv_lora + rope), F32),
                   jax.ShapeDtypeStruct((t, kv_lora), BF16),
                   jax.ShapeDtypeStruct((t, LANES), F32),
                   jax.ShapeDtypeStruct((t, LANES), F32),
                   jax.ShapeDtypeStruct((t, nh), F32)],
        scratch_shapes=[pltpu.VMEM((tm, d), BF16)],
        compiler_params=_cparams("parallel", "arbitrary"),
        name="inproj",
    )(x, sc, sh, nw, wa, wh, qaw, kvw, wkr, wkrs, cos_t, sin_t)


def _q_kernel(cqn_ref, w_ref, wn_ref, wa_ref, wb_ref, c_ref, s_ref, o_ref, *, scale, d_head):
    y = _dot(cqn_ref[...], w_ref[...])
    nope = y[:, :LANES]
    a = y[:, LANES:2 * LANES]
    b = y[:, 2 * LANES:]
    ssq = jnp.sum(nope * nope, axis=-1, keepdims=True) + 0.5 * jnp.sum(a * a, axis=-1, keepdims=True)
    rinv = lax.rsqrt(ssq * (1.0 / d_head) + NORM_EPS) * scale
    o_ref[:, :LANES] = (nope * rinv * wn_ref[...]).astype(o_ref.dtype)
    rot = a * wa_ref[...] * c_ref[...] + b * wb_ref[...] * s_ref[...]
    o_ref[:, LANES:] = (rot * rinv).astype(o_ref.dtype)


def _queries(cqn, wq, wn, wa, wb, tabs, tab_map, *, tm, scale, d_head):
    t, r = cqn.shape
    nheads = wq.shape[0]
    wout = 2 * LANES
    const = lambda i, h: (0, 0)
    kern = functools.partial(_q_kernel, scale=scale, d_head=d_head)
    tab_spec = pl.BlockSpec((tm, LANES), lambda i, h: tab_map(i, h))
    return pl.pallas_call(
        kern,
        grid=(t // tm, nheads),
        in_specs=[pl.BlockSpec((tm, r), lambda i, h: (i, 0)),
                  pl.BlockSpec((None, r, 3 * LANES), lambda i, h: (h, 0, 0)),
                  pl.BlockSpec((1, LANES), const), pl.BlockSpec((1, LANES), const),
                  pl.BlockSpec((1, LANES), const),
                  tab_spec, tab_spec],
        out_specs=pl.BlockSpec((None, tm, wout), lambda i, h: (h, i, 0)),
        out_shape=jax.ShapeDtypeStruct((nheads, t, wout), BF16),
        compiler_params=_cparams("parallel", "parallel"),
        name="queries",
    )(cqn, wq, wn, wa, wb, *tabs)


def _kv_kernel(latb_ref, krr_ref, krsq_ref, w_ref, wn_ref, k_ref, v_ref, *, d_head):
    y = _dot(latb_ref[...], w_ref[...])
    kn = y[:, :LANES]
    ssq = jnp.sum(kn * kn, axis=-1, keepdims=True) + krsq_ref[:, :1]
    rinv = lax.rsqrt(ssq * (1.0 / d_head) + NORM_EPS)
    k_ref[:, :LANES] = (kn * rinv * wn_ref[...]).astype(BF16)
    k_ref[:, LANES:] = (krr_ref[...] * rinv).astype(BF16)
    v_ref[...] = y[:, LANES:].astype(BF16)


def _keys_values(latb, krr, krsq, wkv, wn, *, tm, d_head):
    t, c = latb.shape
    nheads = wkv.shape[0]
    row = lambda i, h: (i, 0)
    return pl.pallas_call(
        functools.partial(_kv_kernel, d_head=d_head),
        grid=(t // tm, nheads),
        in_specs=[pl.BlockSpec((tm, c), row), pl.BlockSpec((tm, LANES), row), pl.BlockSpec((tm, LANES), row),
                  pl.BlockSpec((None, c, 2 * LANES), lambda i, h: (h, 0, 0)),
                  pl.BlockSpec((1, LANES), lambda i, h: (0, 0))],
        out_specs=[pl.BlockSpec((None, tm, 2 * LANES), lambda i, h: (h, i, 0)),
                   pl.BlockSpec((None, tm, LANES), lambda i, h: (h, i, 0))],
        out_shape=[jax.ShapeDtypeStruct((nheads, t, 2 * LANES), BF16),
                   jax.ShapeDtypeStruct((nheads, t, LANES), BF16)],
        compiler_params=_cparams("parallel", "parallel"),
        name="keys_values",
    )(latb, krr, krsq, wkv, wn)


def _flash_kernel(q_ref, k_ref, v_ref, o_ref, m_scr, l_scr, acc_scr, *, tq):
    qi = pl.program_id(2)
    ki = pl.program_id(3)

    @pl.when(ki == 0)
    def _():
        m_scr[...] = jnp.full(m_scr.shape, NEG_INF, F32)
        l_scr[...] = jnp.zeros(l_scr.shape, F32)
        acc_scr[...] = jnp.zeros(acc_scr.shape, F32)

    hps = q_ref.shape[0]
    dv = v_ref.shape[-1]

    def block(on_diagonal):
        for hh in range(hps):
            s = _dot_nt(q_ref[hh], k_ref[hh])
            if on_diagonal:
                qpos = lax.broadcasted_iota(jnp.int32, s.shape, 0)
                kpos = lax.broadcasted_iota(jnp.int32, s.shape, 1)
                s = jnp.where(kpos <= qpos, s, NEG_INF)
            m_old = m_scr[hh]
            m_new = jnp.maximum(m_old, jnp.max(s, axis=-1, keepdims=True))
            alpha = jnp.exp(m_old - m_new)
            p = jnp.exp(s - m_new)
            l_scr[hh] = alpha * l_scr[hh] + jnp.sum(p, axis=-1, keepdims=True)
            acc_scr[hh] = alpha * acc_scr[hh] + _dot(p.astype(BF16), v_ref[hh])
            m_scr[hh] = m_new

    pl.when(ki < qi)(functools.partial(block, False))
    pl.when(ki == qi)(functools.partial(block, True))

    @pl.when(ki == pl.num_programs(3) - 1)
    def _():
        for hh in range(hps):
            o_ref[:, hh * dv:(hh + 1) * dv] = acc_scr[hh] / l_scr[hh]


def _flash(q, k, v, *, nbatch, seq, tq, hps):
    nheads, t, dq = q.shape
    dv = v.shape[-1]
    nq = seq // tq
    return pl.pallas_call(
        functools.partial(_flash_kernel, tq=tq),
        grid=(nbatch, nheads // hps, nq, nq),
        in_specs=[pl.BlockSpec((hps, tq, dq), lambda b, h, qi, ki: (h, b * nq + qi, 0)),
                  pl.BlockSpec((hps, tq, dq), lambda b, h, qi, ki: (h, b * nq + jnp.minimum(ki, qi), 0)),
                  pl.BlockSpec((hps, tq, dv), lambda b, h, qi, ki: (h, b * nq + jnp.minimum(ki, qi), 0))],
        out_specs=pl.BlockSpec((tq, hps * dv), lambda b, h, qi, ki: (b * nq + qi, h)),
        out_shape=jax.ShapeDtypeStruct((t, nheads * dv), F32),
        scratch_shapes=[pltpu.VMEM((hps, tq, 1), F32), pltpu.VMEM((hps, tq, 1), F32),
                        pltpu.VMEM((hps, tq, dv), F32)],
        compiler_params=_cparams("parallel", "parallel", "parallel", "arbitrary"),
        name="prompt_attention",
    )(q, k, v)


def _absorb_kernel(q_ref, wuk_ref, wkn_ref, wkr_ref, o_ref, *, kv_lora):
    q = q_ref[...].astype(F32)
    qn = (q[:, :LANES] * wkn_ref[...]).astype(BF16)
    o_ref[:, :kv_lora] = _dot_nt(qn, wuk_ref[...]).astype(BF16)
    o_ref[:, kv_lora:] = (q[:, LANES:] * wkr_ref[...]).astype(BF16)


def _absorb(q, wuk, wkn, wkr, *, kv_lora):
    nheads, t, wq = q.shape
    wout = kv_lora + LANES
    const = lambda h: (0, 0)
    return pl.pallas_call(
        functools.partial(_absorb_kernel, kv_lora=kv_lora),
        grid=(nheads,),
        in_specs=[pl.BlockSpec((None, t, wq), lambda h: (h, 0, 0)),
                  pl.BlockSpec((None, kv_lora, LANES), lambda h: (h, 0, 0)),
                  pl.BlockSpec((1, LANES), const), pl.BlockSpec((1, LANES), const)],
        out_specs=pl.BlockSpec((t, wout), lambda h: (0, h)),
        out_shape=jax.ShapeDtypeStruct((t, nheads * wout), BF16),
        compiler_params=_cparams("parallel"),
        name="absorb_queries",
    )(q, wuk, wkn, wkr)


def _sattn_kernel(pt_ref, *refs, n_group, kv_lora, rope, nheads, ntok, d_head, page):
    page_refs = refs[:n_group]
    (new_ref, qf_ref, wukt_ref, cs_ref, csn_ref, o_ref,
     lhs_scr, latb_scr, m_scr, l_scr, acc_scr) = refs[n_group:]
    step = pl.program_id(1)
    nrow = nheads * ntok
    nk_w = wukt_ref.shape[0]

    @pl.when(step == 0)
    def _():
        lhs_scr[:nk_w, :] = wukt_ref[...]
        lhs_scr[nk_w:, :] = qf_ref[:, :kv_lora]
        m_scr[...] = jnp.full(m_scr.shape, NEG_INF, F32)
        l_scr[...] = jnp.zeros(l_scr.shape, F32)
        acc_scr[...] = jnp.zeros(acc_scr.shape, F32)

    eye = (lax.broadcasted_iota(jnp.int32, (nrow, nrow), 0)
           == lax.broadcasted_iota(jnp.int32, (nrow, nrow), 1))

    def as_row(col):
        return jnp.sum(jnp.where(eye, col, 0.0), axis=0, keepdims=True)

    def scores(lat_b, kr_t, cs):
        nk = lat_b.shape[1]
        r = _dot(lhs_scr[...], lat_b)
        kn = r[:nk_w]
        kn2 = jnp.sum((kn * kn).reshape(nheads, nk_w // nheads, nk), axis=1)
        krsq = jnp.sum(kr_t * kr_t, axis=0, keepdims=True)
        rinv = lax.rsqrt((kn2 + krsq) * (1.0 / d_head) + NORM_EPS)
        rinv_rows = jnp.concatenate([rinv] * ntok, axis=0)
        f2 = (jnp.concatenate([kr_t, kr_t], axis=0) * cs).astype(BF16)
        return (r[nk_w:] + _dot(qf_ref[:, kv_lora:], f2)) * rinv_rows

    def update(s, lat_b):
        m_old = m_scr[...]
        m_new = jnp.maximum(m_old, jnp.max(s, axis=-1, keepdims=True))
        alpha = jnp.exp(m_old - m_new)
        p = jnp.exp(s - m_new)
        l_scr[...] = alpha * l_scr[...] + jnp.sum(p, axis=-1, keepdims=True)
        acc_scr[...] = acc_scr[...] * as_row(alpha) + _dot_nt(lat_b, p.astype(BF16))
        m_scr[...] = m_new

    per = max(1, min(n_group, MXU_WIDTH // page))
    s_parts = []
    for g in range(0, n_group, per):
        cols = slice(g * page, (g + per) * page)
        lat_b = jnp.concatenate([page_refs[g + u][:kv_lora, :] for u in range(per)], axis=1).astype(BF16)
        kr_t = jnp.concatenate([page_refs[g + u][kv_lora:, :] for u in range(per)], axis=1)
        latb_scr[:, cols] = lat_b
        s_parts.append(scores(lat_b, kr_t, cs_ref[:, cols]))
    update(jnp.concatenate(s_parts, axis=1), latb_scr[...])

    @pl.when(step == pl.num_programs(1) - 1)
    def _():
        key = lax.broadcasted_iota(jnp.int32, (nrow, page), 1)
        tok = lax.broadcasted_iota(jnp.int32, (nrow, page), 0) // nheads
        lat_b = new_ref[:kv_lora, :].astype(BF16)
        s_new = scores(lat_b, new_ref[kv_lora:, :], csn_ref[...])
        update(jnp.where(key <= tok, s_new, NEG_INF), lat_b)
        o_ref[...] = acc_scr[...] / as_row(l_scr[...])


def _sample_attention(page_table, cache_t, new_pages_t, qf, wukt, cs_pages, cs_new, *,
                      n_group, kv_lora, rope, nheads, ntok, d_head):
    nb, n_pages = page_table.shape
    cw, page = cache_t.shape[2], cache_t.shape[3]
    nrow = nheads * ntok
    wqf = qf.shape[-1]
    nsteps = n_pages // n_group

    def page_map(g):
        return lambda b, s, pt: (0, pt[b * n_pages + s * n_group + g], 0, 0)

    kern = functools.partial(_sattn_kernel, n_group=n_group, kv_lora=kv_lora, rope=rope, nheads=nheads,
                             ntok=ntok, d_head=d_head, page=page)
    in_specs = [pl.BlockSpec((None, None, cw, page), page_map(g)) for g in range(n_group)]
    in_specs += [pl.BlockSpec((None, cw, page), lambda b, s, pt: (b, 0, 0)),
                 pl.BlockSpec((None, nrow, wqf), lambda b, s, pt: (b, 0, 0)),
                 pl.BlockSpec(wukt.shape, lambda b, s, pt: (0, 0)),
                 pl.BlockSpec((2 * rope, n_group * page), lambda b, s, pt: (0, s)),
                 pl.BlockSpec((2 * rope, page), lambda b, s, pt: (0, 0))]
    grid_spec = pltpu.PrefetchScalarGridSpec(
        num_scalar_prefetch=1,
        grid=(nb, nsteps),
        in_specs=in_specs,
        out_specs=pl.BlockSpec((None, kv_lora, nrow), lambda b, s, pt: (b, 0, 0)),
        scratch_shapes=[pltpu.VMEM((wukt.shape[0] + nrow, kv_lora), BF16),
                        pltpu.VMEM((kv_lora, n_group * page), BF16),
                        pltpu.VMEM((nrow, 1), F32), pltpu.VMEM((nrow, 1), F32),
                        pltpu.VMEM((kv_lora, nrow), F32)])
    return pl.pallas_call(
        kern,
        grid_spec=grid_spec,
        out_shape=jax.ShapeDtypeStruct((nb, kv_lora, nrow), F32),
        compiler_params=_cparams("parallel", "arbitrary"),
        name="sample_attention",
    )(page_table.reshape(-1), *([cache_t] * n_group), new_pages_t, qf, wukt, cs_pages, cs_new)


def _uv_kernel(lm_ref, w_ref, o_ref):
    o_ref[...] = _dot(lm_ref[...].astype(BF16), w_ref[...])


def _uv_project(lm, wuv):
    t = lm.shape[0]
    nheads, c, dv = wuv.shape
    return pl.pallas_call(
        _uv_kernel,
        grid=(nheads,),
        in_specs=[pl.BlockSpec((t, c), lambda h: (0, h)),
                  pl.BlockSpec((None, c, dv), lambda h: (h, 0, 0))],
        out_specs=pl.BlockSpec((t, dv), lambda h: (0, h)),
        out_shape=jax.ShapeDtypeStruct((t, nheads * dv), F32),
        compiler_params=_cparams("parallel"),
        name="value_up_projection",
    )(lm, wuv)


def _cumsum_rows(g, chunk):
    if chunk <= SUBLANES:
        ridx = lax.broadcasted_iota(jnp.int32, g.shape, 0)
        out = jnp.zeros_like(g)
        for s in range(chunk):
            out = out + jnp.where(ridx >= s, g[s:s + 1, :], 0.0)
        return out
    rows = lax.broadcasted_iota(jnp.int32, (chunk, chunk), 0)
    cols = lax.broadcasted_iota(jnp.int32, (chunk, chunk), 1)
    tri = (cols <= rows).astype(BF16)
    hi = g.astype(BF16)
    r1 = g - hi.astype(F32)
    mid = r1.astype(BF16)
    lo = (r1 - mid.astype(F32)).astype(BF16)
    return _dot(tri, hi) + (_dot(tri, mid) + _dot(tri, lo))


def _hgrn_kernel(hq_ref, hf_ref, hi_ref, hg_ref, lbp_ref, s0_ref, gnw_ref, o_ref, sout_ref, st_scr, *,
                 chunk, sub, nchunk, t_valid, has_s0, hps):
    tb = pl.program_id(2)
    last = tb == pl.num_programs(2) - 1
    gnw = gnw_ref[...]

    @pl.when(tb == 0)
    def _():
        for hh in range(hps):
            if has_s0:
                st_scr[hh] = s0_ref[hh].T
            else:
                st_scr[hh] = jnp.zeros(st_scr.shape[1:], F32)

    finals = []
    for hh in range(hps):
        cs = slice(hh * LANES, (hh + 1) * LANES)

        a = lbp_ref[:, cs]
        e = jnp.exp(a - jnp.max(a, axis=0, keepdims=True))
        lb = e[0:1, :] / jnp.sum(e, axis=0, keepdims=True)

        st = st_scr[hh]
        for c in range(nchunk):
            sl = slice(c * chunk, (c + 1) * chunk)
            zq = hq_ref[sl, cs]
            q = zq * _sigmoid(zq)
            k = (1.0 - lb) * _sigmoid(-hf_ref[sl, cs])
            g = jnp.log(1.0 - k)
            if t_valid is not None:
                valid = (tb * (nchunk * chunk) + c * chunk
                         + lax.broadcasted_iota(jnp.int32, k.shape, 0)) < t_valid
                k = jnp.where(valid, k, 0.0)
                g = jnp.where(valid, g, 0.0)
            v = hi_ref[sl, cs]
            zg = hg_ref[sl, cs]
            gate = zg * _sigmoid(zg)

            gc = _cumsum_rows(g, chunk)
            o = _dot_nt(q * jnp.exp(gc), st)
            g_last = gc[chunk - 1:chunk, :]
            u_t = _dot_tn(v, k * jnp.exp(g_last - gc))

            parts = []
            for i in range(chunk // sub):
                r0 = i * sub
                gi = gc[r0:r0 + sub, :]
                qi = q[r0:r0 + sub, :]
                if i > 0:
                    ref_row = gc[r0 - 1:r0, :]
                    a_off = _dot_nt(qi * jnp.exp(gi - ref_row), k[:r0, :] * jnp.exp(ref_row - gc[:r0, :]))
                    o_i = _dot(a_off, v[:r0, :])
                else:
                    o_i = jnp.zeros((sub, v.shape[1]), F32)
                trow = lax.broadcasted_iota(jnp.int32, (sub, 1), 0)
                for s in range(sub):
                    keep = trow >= s
                    d = jnp.where(keep, gi - gc[r0 + s:r0 + s + 1, :], 0.0)
                    w = jnp.exp(d) * qi * k[r0 + s:r0 + s + 1, :]
                    a_col = jnp.where(keep, jnp.sum(w, axis=-1, keepdims=True), 0.0)
                    o_i = o_i + a_col * v[r0 + s:r0 + s + 1, :]
                parts.append(o_i)
            o = o + (parts[0] if len(parts) == 1 else jnp.concatenate(parts, axis=0))
            st = st * jnp.exp(g_last) + u_t
            o_ref[sl, cs] = _rms(o) * gnw * gate

        st_scr[hh] = st
        finals.append(st)

    @pl.when(last)
    def _():
        for hh in range(hps):
            sout_ref[hh] = finals[hh].T


def _hgrn(zh, lbp, s0, gnw, *, nheads, hps, tblock, chunk, sub, t_valid):
    n, t, _ = zh.shape
    dk = LANES
    has_s0 = s0 is not None
    nlb = lbp.shape[0]
    ngrp = nheads // hps
    if not has_s0:
        s0 = jnp.zeros((1, hps, dk, dk), F32)
        s0_spec = pl.BlockSpec((None, hps, dk, dk), lambda b, h, tb: (0, 0, 0, 0))
    else:
        s0_spec = pl.BlockSpec((None, hps, dk, dk), lambda b, h, tb: (b, h, 0, 0))

    def col(group):
        return pl.BlockSpec((None, tblock, hps * dk), lambda b, h, tb: (b, tb, group * ngrp + h))

    kern = functools.partial(_hgrn_kernel, chunk=chunk, sub=sub, nchunk=tblock // chunk, t_valid=t_valid,
                             has_s0=has_s0, hps=hps)
    return pl.pallas_call(
        kern,
        grid=(n, ngrp, t // tblock),
        in_specs=[col(0), col(1), col(2), col(3),
                  pl.BlockSpec((nlb, hps * dk), lambda b, h, tb: (0, h)),
                  s0_spec,
                  pl.BlockSpec((1, dk), lambda b, h, tb: (0, 0))],
        out_specs=[pl.BlockSpec((None, tblock, hps * dk), lambda b, h, tb: (b, tb, h)),
                   pl.BlockSpec((None, hps, dk, dk), lambda b, h, tb: (b, h, 0, 0))],
        out_shape=[jax.ShapeDtypeStruct((n, t, nheads * dk), F32),
                   jax.ShapeDtypeStruct((n, nheads, dk, dk), F32)],
        scratch_shapes=[pltpu.VMEM((hps, dk, dk), F32)],
        compiler_params=_cparams("parallel", "parallel", "arbitrary"),
        name="hgrn2",
    )(zh, zh, zh, zh, lbp, s0, gnw)


def _outproj_kernel(oa_ref, orec_ref, x_ref, ga_ref, scf_ref, shf_ref, gnw_ref, fnw_ref, wo_ref, rw_ref,
                    rb_ref, x1_ref, h2_ref, topi_ref, gate_ref, *, n_attn):
    oan = (_rms(oa_ref[...]) * gnw_ref[...]).astype(BF16)
    mix = _dot(oan, wo_ref[:n_attn, :]) + _dot(orec_ref[...].astype(BF16), wo_ref[n_attn:, :])
    x1 = x_ref[...] + ga_ref[...] * mix
    x1_ref[...] = x1
    h2 = _rms(x1) * fnw_ref[...] * (1.0 + scf_ref[...]) + shf_ref[...]
    h2_ref[...] = h2
    logits = jnp.dot(h2, rw_ref[...], preferred_element_type=F32,
                     precision=lax.Precision.HIGHEST) + rb_ref[...]
    lane = lax.broadcasted_iota(jnp.int32, logits.shape, 1).astype(F32)
    vals = []
    topi = jnp.zeros(logits.shape, F32)
    work = logits
    for kk in range(TOP_K):
        m = jnp.max(work, axis=-1, keepdims=True)
        idx = jnp.min(jnp.where(work == m, lane, float(LANES)), axis=-1, keepdims=True)
        vals.append(m)
        topi = jnp.where(lane == float(kk), idx, topi)
        work = jnp.where(lane == idx, -3e38, work)
    es = [jnp.exp(vv - vals[0]) for vv in vals]
    den = es[0]
    for ee in es[1:]:
        den = den + ee
    gates = jnp.zeros(logits.shape, F32)
    for kk in range(TOP_K):
        gates = jnp.where(lane == float(kk), es[kk] / den, gates)
    topi_ref[...] = topi.astype(jnp.int32)
    gate_ref[...] = gates


def _outproj(oa, orec, x, ga, scf, shf, mod_spec, gnw, fnw, wo, rw, rb, *, tm):
    t, d = x.shape
    n_attn = oa.shape[1]
    n_rec = orec.shape[1]
    row = lambda i: (i, 0)
    const = lambda i: (0, 0)
    return pl.pallas_call(
        functools.partial(_outproj_kernel, n_attn=n_attn),
        grid=(t // tm,),
        in_specs=[pl.BlockSpec((tm, n_attn), row), pl.BlockSpec((tm, n_rec), row), pl.BlockSpec((tm, d), row),
                  mod_spec, mod_spec, mod_spec,
                  pl.BlockSpec((1, n_attn), const), pl.BlockSpec((1, d), const),
                  pl.BlockSpec(wo.shape, const), pl.BlockSpec(rw.shape, const), pl.BlockSpec((1, LANES), const)],
        out_specs=[pl.BlockSpec((tm, d), row), pl.BlockSpec((tm, d), row),
                   pl.BlockSpec((tm, LANES), row), pl.BlockSpec((tm, LANES), row)],
        out_shape=[jax.ShapeDtypeStruct((t, d), F32), jax.ShapeDtypeStruct((t, d), F32),
                   jax.ShapeDtypeStruct((t, LANES), jnp.int32), jax.ShapeDtypeStruct((t, LANES), F32)],
        compiler_params=_cparams("parallel"),
        name="outproj_router",
    )(oa, orec, x, ga, scf, shf, gnw, fnw, wo, rw, rb)


def _gather_kernel(idx_ref, src_ref, dst_ref, sem, *, rows_per_step):
    def row_copy(r, src_row):
        return pltpu.make_async_copy(src_ref.at[src_row], dst_ref.at[r], sem)

    def issue(r, carry):
        row_copy(r, idx_ref[r]).start()
        return carry

    def drain(r, carry):
        row_copy(r, 0).wait()
        return carry

    lax.fori_loop(0, rows_per_step, issue, 0)
    lax.fori_loop(0, rows_per_step, drain, 0)


def _gather_rows(idx, src, *, rows_per_step):
    n = idx.shape[0]
    return pl.pallas_call(
        functools.partial(_gather_kernel, rows_per_step=rows_per_step),
        grid=(n // rows_per_step,),
        in_specs=[pl.BlockSpec((rows_per_step,), lambda i: (i,), memory_space=pltpu.SMEM),
                  pl.BlockSpec(memory_space=pl.ANY)],
        out_specs=pl.BlockSpec((rows_per_step,) + src.shape[1:], lambda i: (i, 0, 0)),
        out_shape=jax.ShapeDtypeStruct((n,) + src.shape[1:], src.dtype),
        scratch_shapes=[pltpu.SemaphoreType.DMA(())],
        compiler_params=_cparams("parallel"),
        name="gather_rows",
    )(idx, src)


def _moe_kernel(te_ref, nu_ref, idx0_ref, idxn_ref, h_ref, w1g_ref, w1l_ref, b1g_ref, b1l_ref, w2_ref, b2_ref,
                o_ref, xbuf, xb_scr, acc_scr, sem, *, nseg, nj):
    i = pl.program_id(0)
    j = pl.program_id(1)
    tm = xb_scr.shape[0]
    npk = nseg // 2
    half = npk * LANES
    per = tm // nj
    nu = nu_ref[0]

    def row_copy(slot, r, src_row):
        dst = xbuf.at[pl.ds(pl.multiple_of((slot * tm + r) * npk, npk), npk), :]
        return pltpu.make_async_copy(h_ref.at[src_row], dst, sem.at[slot])

    def issue_rows(slot, idx_ref, start, count):
        def body(r, carry):
            row_copy(slot, r, idx_ref[r]).start()
            return carry
        lax.fori_loop(start, start + count, body, 0)

    def wait_rows(slot):
        def body(r, carry):
            row_copy(slot, r, 0).wait()
            return carry
        lax.fori_loop(0, tm, body, 0)

    @pl.when(jnp.logical_and(i == 0, j == 0))
    def _():
        issue_rows(0, idx0_ref, 0, tm)

    for slot in (0, 1):
        @pl.when(jnp.logical_and(i % 2 == slot, i < nu))
        def _():
            @pl.when(i + 1 < nu)
            def _():
                issue_rows(1 - slot, idxn_ref, j * per, per)

            @pl.when(j == 0)
            def _():
                wait_rows(slot)
                for s in range(npk):
                    u = xbuf[pl.ds(slot * tm * npk + s, tm, stride=npk), :]
                    lo = lax.bitcast_convert_type(u << 16, F32)
                    hi = lax.bitcast_convert_type(u & jnp.uint32(0xFFFF0000), F32)
                    xb_scr[:, s * LANES:(s + 1) * LANES] = lo.astype(BF16)
                    xb_scr[:, half + s * LANES:half + (s + 1) * LANES] = hi.astype(BF16)
                acc_scr[...] = jnp.broadcast_to(b2_ref[...], acc_scr.shape)

    @pl.when(i < nu)
    def _():
        xb = xb_scr[...]
        yg = _dot(xb, w1g_ref[...].astype(BF16)) + b1g_ref[...]
        yl = _dot(xb, w1l_ref[...].astype(BF16)) + b1l_ref[...]
        glu = jnp.minimum(yg, SWIGLU_LIMIT)
        lin = jnp.clip(yl, -SWIGLU_LIMIT, SWIGLU_LIMIT)
        a = (glu * _sigmoid(SWIGLU_ALPHA * glu) * (lin + 1.0)).astype(BF16)
        acc_scr[...] += _dot(a, w2_ref[...].astype(BF16))

        @pl.when(j == nj - 1)
        def _():
            for s in range(nseg):
                o_ref[pl.ds(s, tm, stride=nseg), :] = acc_scr[:, s * LANES:(s + 1) * LANES]

    @pl.when(jnp.logical_and(i >= nu, j == 0))
    def _():
        o_ref[...] = jnp.zeros(o_ref.shape, F32)


def _moe_ffn(tile_e, n_used, slot_tok, h2p, w1, b1, w2, b2, *, tm, th, nseg):
    d = nseg * LANES
    p = slot_tok.shape[0]
    ne, _, two_de = w1.shape
    de = two_de // 2
    nj = de // th
    ntiles = p // tm
    assert tm % nj == 0 and nseg % 2 == 0

    def tile(i, nu):
        return jnp.minimum(i, nu[0] - 1)

    def hid(i, j, nu):
        return jnp.where(i < nu[0], j, nj - 1)

    grid_spec = pltpu.PrefetchScalarGridSpec(
        num_scalar_prefetch=2,
        grid=(ntiles, nj),
        in_specs=[pl.BlockSpec((tm,), lambda i, j, te, nu: (0,), memory_space=pltpu.SMEM),
                  pl.BlockSpec((tm,), lambda i, j, te, nu: (jnp.minimum(i + 1, ntiles - 1),),
                               memory_space=pltpu.SMEM),
                  pl.BlockSpec(memory_space=pl.ANY),
                  pl.BlockSpec((None, d, th), lambda i, j, te, nu: (te[tile(i, nu)], 0, hid(i, j, nu))),
                  pl.BlockSpec((None, d, th), lambda i, j, te, nu: (te[tile(i, nu)], 0, nj + hid(i, j, nu))),
                  pl.BlockSpec((None, 1, th), lambda i, j, te, nu: (te[tile(i, nu)], 0, hid(i, j, nu))),
                  pl.BlockSpec((None, 1, th), lambda i, j, te, nu: (te[tile(i, nu)], 0, nj + hid(i, j, nu))),
                  pl.BlockSpec((None, th, d), lambda i, j, te, nu: (te[tile(i, nu)], hid(i, j, nu), 0)),
                  pl.BlockSpec((None, 1, d), lambda i, j, te, nu: (te[tile(i, nu)], 0, 0))],
        out_specs=pl.BlockSpec((tm * nseg, LANES), lambda i, j, te, nu: (i, 0)),
        scratch_shapes=[pltpu.VMEM((2 * tm * (nseg // 2), LANES), jnp.uint32),
                        pltpu.VMEM((tm, d), BF16),
                        pltpu.VMEM((tm, d), F32),
                        pltpu.SemaphoreType.DMA((2,))])
    return pl.pallas_call(
        functools.partial(_moe_kernel, nseg=nseg, nj=nj),
        grid_spec=grid_spec,
        out_shape=jax.ShapeDtypeStruct((p * nseg, LANES), F32),
        compiler_params=_cparams("arbitrary", "arbitrary"),
        name="moe_experts",
    )(tile_e, n_used, slot_tok, slot_tok, h2p, w1, w1, b1.reshape(ne, 1, two_de),
      b1.reshape(ne, 1, two_de), w2, b2.reshape(ne, 1, d))


def _combine_kernel(*refs, nseg):
    y_refs = refs[:TOP_K]
    gate_ref, x1_ref, gf_ref, o_ref = refs[TOP_K:]
    tm = x1_ref.shape[0]
    gates = gate_ref[...]
    gk = [gates[:, kk:kk + 1] for kk in range(TOP_K)]
    for s in range(nseg):
        seg = slice(s * LANES, (s + 1) * LANES)
        acc = gk[0] * y_refs[0][pl.ds(s, tm, stride=nseg), :]
        for kk in range(1, TOP_K):
            acc = acc + gk[kk] * y_refs[kk][pl.ds(s, tm, stride=nseg), :]
        o_ref[:, seg] = x1_ref[:, seg] + gf_ref[:, seg] * acc


def _combine(yg2, gates, x1, gf, mod_spec, *, tm, row_off, nblk_all, nseg):
    t, d = x1.shape

    def ymap(kk):
        return lambda i: (kk * nblk_all + row_off + i, 0)

    return pl.pallas_call(
        functools.partial(_combine_kernel, nseg=nseg),
        grid=(t // tm,),
        in_specs=[pl.BlockSpec((tm * nseg, LANES), ymap(kk)) for kk in range(TOP_K)] + [
            pl.BlockSpec((tm, LANES), lambda i: (i + row_off, 0)),
            pl.BlockSpec((tm, d), lambda i: (i, 0)),
            mod_spec],
        out_specs=pl.BlockSpec((tm, d), lambda i: (i, 0)),
        out_shape=jax.ShapeDtypeStruct((t, d), F32),
        compiler_params=_cparams("parallel"),
        name="moe_combine",
    )(*([yg2] * TOP_K), gates, x1, gf)


def _rope_angles(pos, half):
    inv = ROPE_THETA ** (-jnp.arange(half, dtype=F32) / half)
    ang = pos.astype(F32)[:, None] * inv[None, :]
    return jnp.cos(ang), jnp.sin(ang)


def _pad_lanes(v, width=LANES):
    v = v.reshape(1, -1).astype(F32)
    return jnp.pad(v, ((0, 0), (0, width - v.shape[1])))


def _swap_halves(v):
    h = v.shape[-1] // 2
    return jnp.concatenate([v[..., h:], v[..., :h]], -1)


def kernel(x_prompt, x_sample, cache_mla, state_hgrn, page_table, c_prompt, c_sample, w_ada, b_ada,
           attn_norm_w, w_in, q_a_norm_w, w_uq, kv_a_norm_w, w_ukv, q_head_norm_w, k_head_norm_w,
           attn_group_norm_w, hg_lower_bound, hg_norm_w, w_out, ffn_norm_w, router_w, router_b,
           w1, b1, w2, b2):
    nb_p, seq, d = x_prompt.shape
    nb_s, ntok, _ = x_sample.shape
    depth = w_ada.shape[0]
    assert depth == 1, "single-layer step"
    page = cache_mla.shape[2]
    cache_w = cache_mla.shape[3]
    n_pages = page_table.shape[1]
    past_len = n_pages * page
    q_lora, nheads, d_head = w_uq.shape[1], w_uq.shape[2], w_uq.shape[3]
    kv_lora = w_ukv.shape[1]
    rope = cache_w - kv_lora
    nope = d_head - rope
    v_head = w_ukv.shape[3] - nope
    hg_heads, hg_dk, hg_dv = state_hgrn.shape[2], state_hgrn.shape[3], state_hgrn.shape[4]
    n_experts = router_w.shape[2]
    assert nope == LANES and v_head == LANES and hg_dk == LANES and hg_dv == LANES
    assert 2 * rope == LANES and n_experts <= LANES and d % LANES == 0
    half = rope // 2
    nseg = d // LANES
    attn_scale = d_head ** -0.5
    t_p = nb_p * seq
    t_s = nb_s * ntok

    mod = _adaln(jnp.concatenate([c_prompt, c_sample], 0), w_ada[0], b_ada[0])
    mods = [mod[:, i * d:(i + 1) * d] for i in range(6)]
    mods_p = [m[:nb_p].reshape(nb_p, 1, d) for m in mods]
    mods_s = [jnp.repeat(m[nb_p:], ntok, axis=0) for m in mods]

    wi = w_in[0]
    o_kr = q_lora + kv_lora
    w_kr = wi[:, o_kr:o_kr + rope]
    zpad = jnp.zeros((d, LANES - rope), F32)
    wa = jnp.concatenate([wi[:, :o_kr], w_kr, zpad, _swap_halves(w_kr), zpad], 1).astype(BF16)
    wh = wi[:, o_kr + rope:].astype(BF16)
    wq_full = jnp.transpose(w_uq[0], (1, 0, 2))
    wq_r = wq_full[:, :, nope:]
    wq_rs = _swap_halves(wq_r)
    wq = jnp.concatenate([wq_full[:, :, :nope], wq_r, wq_rs, wq_rs, wq_r], -1).astype(BF16)
    wkv = jnp.transpose(w_ukv[0], (1, 0, 2)).astype(BF16)
    wuk = wkv[:, :, :nope]
    wuv = wkv[:, :, nope:]
    wukt = jnp.transpose(wuk, (0, 2, 1)).reshape(nheads * nope, kv_lora)
    qn_w, kn_w = q_head_norm_w[0], k_head_norm_w[0]
    qr_w, kr_w = qn_w[nope:], kn_w[nope:]
    qw_n = _pad_lanes(qn_w[:nope])
    qw_a = jnp.concatenate([qr_w, _swap_halves(qr_w)]).reshape(1, LANES)
    qw_b = jnp.concatenate([_swap_halves(qr_w), qr_w]).reshape(1, LANES)
    kw_n, kw_a, kw_b = _pad_lanes(kn_w[:nope]), _pad_lanes(kr_w), _pad_lanes(_swap_halves(kr_w))
    kw_rr = jnp.concatenate([kr_w, kr_w]).reshape(1, LANES)
    nw = attn_norm_w[0].reshape(1, d)
    qaw = q_a_norm_w[0].reshape(1, q_lora)
    kvw = kv_a_norm_w[0].reshape(1, kv_lora)

    def token_tables(pos):
        cos, sin = _rope_angles(pos, half)
        z = jnp.zeros((pos.shape[0], LANES - rope), F32)
        k_cos = jnp.concatenate([cos, cos, z], -1)
        k_sin = jnp.concatenate([-sin, sin, z], -1)
        q_cos = jnp.concatenate([cos, cos, cos, -cos], -1)
        q_sin = jnp.concatenate([-sin, sin, sin, sin], -1)
        return k_cos, k_sin, q_cos, q_sin

    tabs_p = token_tables(jnp.arange(seq))
    tabs_s = token_tables(past_len + (jnp.arange(t_s) % ntok))

    tm_p = min(512, seq)
    tm_s = t_s
    nblk = seq // tm_p
    modspec_p2 = pl.BlockSpec((None, 1, d), lambda i, j: (i // nblk, 0, 0))
    modspec_s2 = pl.BlockSpec((tm_s, d), lambda i, j: (i, 0))
    dims = dict(q_lora=q_lora, kv_lora=kv_lora, rope=rope)

    xp = x_prompt.reshape(t_p, d)
    xs = x_sample.reshape(t_s, d)
    cqn_p, rows_p, latb_p, krr_p, krsq_p, zh_p = _inproj(
        xp, mods_p[1], mods_p[0], modspec_p2, nw, wa, wh, qaw, kvw, kw_a, kw_b, tabs_p[0], tabs_p[1],
        lambda i, j: (i % nblk, 0), tm=tm_p, **dims)
    cqn_s, rows_s, latb_s, krr_s, krsq_s, zh_s = _inproj(
        xs, mods_s[1], mods_s[0], modspec_s2, nw, wa, wh, qaw, kvw, kw_a, kw_b, tabs_s[0], tabs_s[1],
        lambda i, j: (i, 0), tm=tm_s, **dims)

    q_p = _queries(cqn_p, wq, qw_n, qw_a, qw_b, tabs_p[2:], lambda i, h: (i % nblk, 0), tm=tm_p,
                   scale=attn_scale, d_head=d_head)
    k_p, v_p = _keys_values(latb_p, krr_p, krsq_p, wkv, kw_n, tm=tm_p, d_head=d_head)
    oa_p = _flash(q_p, k_p, v_p, nbatch=nb_p, seq=seq, tq=tm_p, hps=2 if nheads % 2 == 0 else 1)

    q_s = _queries(cqn_s, wq, qw_n, qw_a, qw_b, tabs_s[2:], lambda i, h: (i, 0), tm=tm_s,
                   scale=attn_scale, d_head=d_head)
    qf = _absorb(q_s, wuk, kw_n, kw_rr, kv_lora=kv_lora)
    qf = qf.reshape(nb_s, ntok * nheads, kv_lora + LANES)

    def cs_table(pos):
        cos, sin = _rope_angles(pos, half)
        return jnp.concatenate([cos, cos, sin, sin], -1).T

    cs_pages = cs_table(jnp.arange(past_len))
    cs_new = cs_table(past_len + jnp.arange(page))
    cache_t = jnp.swapaxes(cache_mla, 2, 3)
    new_pages = jnp.zeros((nb_s, page, cache_w), F32).at[:, :ntok].set(rows_s.reshape(nb_s, ntok, cache_w))
    new_pages_t = jnp.swapaxes(new_pages, 1, 2)
    n_group = 8
    while n_pages % n_group:
        n_group //= 2
    lat_mix = _sample_attention(page_table, cache_t, new_pages_t, qf, wukt, cs_pages, cs_new,
                                n_group=n_group, kv_lora=kv_lora, rope=rope, nheads=nheads, ntok=ntok,
                                d_head=d_head)
    lat_mix = jnp.swapaxes(lat_mix, 1, 2)
    oa_s = _uv_project(lat_mix.reshape(t_s, nheads * kv_lora), wuv)

    gnw_h = hg_norm_w[0].reshape(1, hg_dv)
    chunk_p = min(HG_CHUNK, seq)
    orec_p, st_p = _hgrn(zh_p.reshape(nb_p, seq, -1), hg_lower_bound, None, gnw_h, nheads=hg_heads,
                         hps=2 if hg_heads % 2 == 0 else 1,
                         tblock=min(256, seq), chunk=chunk_p, sub=min(HG_SUB, chunk_p), t_valid=None)
    t_pad = -(-ntok // SUBLANES) * SUBLANES
    zh_s3 = jnp.pad(zh_s.reshape(nb_s, ntok, -1), ((0, 0), (0, t_pad - ntok), (0, 0)))
    orec_s, st_s = _hgrn(zh_s3, hg_lower_bound, state_hgrn[0], gnw_h, nheads=hg_heads, hps=hg_heads,
                         tblock=t_pad, chunk=t_pad, sub=t_pad, t_valid=ntok)
    orec_s = orec_s[:, :ntok].reshape(t_s, -1)

    wo = w_out[0].astype(BF16)
    rw = jnp.pad(router_w[0], ((0, 0), (0, LANES - n_experts)))
    rb = jnp.concatenate([router_b[0], jnp.full((LANES - n_experts,), NEG_INF, F32)]).reshape(1, LANES)
    gnw_a = attn_group_norm_w[0].reshape(1, -1)
    fnw = ffn_norm_w[0].reshape(1, d)
    modspec_p1 = pl.BlockSpec((None, 1, d), lambda i: (i // nblk, 0, 0))
    modspec_s1 = pl.BlockSpec((tm_s, d), lambda i: (i, 0))
    x1_p, h2_p, ti_p, gt_p = _outproj(oa_p, orec_p.reshape(t_p, -1), xp, mods_p[2], mods_p[4], mods_p[3],
                                      modspec_p1, gnw_a, fnw, wo, rw, rb, tm=tm_p)
    x1_s, h2_s, ti_s, gt_s = _outproj(oa_s, orec_s, xs, mods_s[2], mods_s[4], mods_s[3],
                                      modspec_s1, gnw_a, fnw, wo, rw, rb, tm=tm_s)

    n_tok = t_p + t_s
    h2b = jnp.concatenate([h2_p, h2_s], 0).astype(BF16)
    h2w = lax.bitcast_convert_type(h2b, jnp.uint16).astype(jnp.uint32)
    h2p = (h2w[:, :d // 2] | (h2w[:, d // 2:] << 16)).reshape(n_tok, nseg // 2, LANES)
    top_i = jnp.concatenate([ti_p[:, :TOP_K], ti_s[:, :TOP_K]], 0)
    gates = jnp.concatenate([gt_p, gt_s], 0)
    rows_per_step = 1024
    tm_e = rows_per_step if n_tok >= rows_per_step else 1 << (n_tok.bit_length() - 1)
    n_assign = n_tok * TOP_K
    flat_e = top_i.reshape(-1)
    onehot = (flat_e[:, None] == jnp.arange(n_experts, dtype=jnp.int32)[None, :]).astype(jnp.int32)
    csum = jnp.cumsum(onehot, axis=0)
    counts = csum[-1]
    rank = jnp.take_along_axis(csum, flat_e[:, None], axis=1)[:, 0] - 1
    padded = (counts + tm_e - 1) // tm_e * tm_e
    pend = jnp.cumsum(padded)
    pstart = pend - padded
    dest = (pstart[flat_e] + rank).astype(jnp.int32)
    n_slots = -(-(n_assign + n_experts * tm_e) // rows_per_step) * rows_per_step
    n_slots = -(-n_slots // tm_e) * tm_e
    flat_tok = jnp.arange(n_assign, dtype=jnp.int32) // TOP_K
    slot_tok = jnp.zeros((n_slots,), jnp.int32).at[dest].set(flat_tok, unique_indices=True)
    n_tiles = n_slots // tm_e
    tile_e = jnp.minimum(jnp.searchsorted(pend, jnp.arange(n_tiles, dtype=jnp.int32) * tm_e, side='right'),
                         n_experts - 1).astype(jnp.int32)
    n_used = (pend[-1] // tm_e).astype(jnp.int32).reshape(1)

    ys2 = _moe_ffn(tile_e, n_used, slot_tok, h2p, w1[0], b1[0], w2[0], b2[0],
                   tm=tm_e, th=min(256, d), nseg=nseg)
    tm_c = min(512, t_s)
    assert n_tok % tm_c == 0 and t_p % tm_c == 0
    dest_kmajor = dest.reshape(n_tok, TOP_K).T.reshape(-1)
    n_pad = -(-n_assign // rows_per_step) * rows_per_step
    dest_kmajor = jnp.pad(dest_kmajor, (0, n_pad - n_assign))
    yg = _gather_rows(dest_kmajor, ys2.reshape(n_slots, nseg, LANES), rows_per_step=rows_per_step)
    yg2 = yg.reshape(n_pad * nseg, LANES)

    nblk_all = n_tok // tm_c
    y_p = _combine(yg2, gates, x1_p, mods_p[5], pl.BlockSpec((None, 1, d), lambda i: (i // (seq // tm_c), 0, 0)),
                   tm=tm_c, row_off=0, nblk_all=nblk_all, nseg=nseg)
    y_s = _combine(yg2, gates, x1_s, mods_s[5], pl.BlockSpec((tm_c, d), lambda i: (i, 0)),
                   tm=tm_c, row_off=t_p // tm_c, nblk_all=nblk_all, nseg=nseg)

    return (y_p.reshape(nb_p, seq, d), y_s.reshape(nb_s, ntok, d),
            rows_p.reshape(1, nb_p, seq, cache_w), rows_s.reshape(1, nb_s, ntok, cache_w),
            st_p[None], st_s[None])
```

```python
import functools

import jax
import jax.numpy as jnp
from jax import lax
from jax.experimental import pallas as pl
from jax.experimental.pallas import tpu as pltpu

F32 = jnp.float32
BF16 = jnp.bfloat16

NORM_EPS = 1e-6
NEG_INF = -1e30
ROPE_THETA = 10000.0
TOP_K = 4
SWIGLU_LIMIT = 7.0
SWIGLU_ALPHA = 1.702
HG_CHUNK = 64
HG_SUB = 16
LANES = 128
SUBLANES = 8
MXU_WIDTH = 256
VMEM_LIMIT_BYTES = 56 * 1024 * 1024


def _cparams(*sem):
    return pltpu.CompilerParams(dimension_semantics=sem, vmem_limit_bytes=VMEM_LIMIT_BYTES)


def _pick(n, pref, mult=LANES):
    if n <= pref:
        return n
    t = pref - pref % mult
    while t > mult and n % t:
        t -= mult
    assert n % t == 0, (n, pref, mult)
    return t


def _sigmoid(x):
    return 1.0 / (1.0 + jnp.exp(-x))


def _rms(x, eps=NORM_EPS):
    return x * lax.rsqrt(jnp.mean(x * x, axis=-1, keepdims=True) + eps)


def _dot(a, b):
    return jnp.dot(a, b, preferred_element_type=F32)


def _dot_nt(a, b):
    return lax.dot_general(a, b, (((1,), (1,)), ((), ())), preferred_element_type=F32)


def _dot_tn(a, b):
    return lax.dot_general(a, b, (((0,), (0,)), ((), ())), preferred_element_type=F32)


def _adaln_kernel(c_ref, w_ref, b_ref, o_ref):
    c = c_ref[...]
    a = (c * _sigmoid(c)).astype(BF16)
    o_ref[...] = _dot(a, w_ref[...].astype(BF16)) + b_ref[...]


def _adaln(c, w, b):
    n, d = c.shape
    nout = w.shape[1]
    tn = _pick(nout, 1024)
    return pl.pallas_call(
        _adaln_kernel,
        grid=(nout // tn,),
        in_specs=[pl.BlockSpec((n, d), lambda j: (0, 0)),
                  pl.BlockSpec((d, tn), lambda j: (0, j)),
                  pl.BlockSpec((1, tn), lambda j: (0, j))],
        out_specs=pl.BlockSpec((n, tn), lambda j: (0, j)),
        out_shape=jax.ShapeDtypeStruct((n, nout), F32),
        compiler_params=_cparams("parallel"),
        name="adaln",
    )(c, w, b.reshape(1, nout))


def _inproj_kernel(x_ref, sc_ref, sh_ref, nw_ref, wa_ref, wh_ref, qaw_ref, kvw_ref, wkr_ref, wkrs_ref,
                   cos_ref, sin_ref,
                   cqn_ref, rows_ref, latb_ref, krr_ref, krsq_ref, zh_ref, h_scr, *, q_lora, kv_lora, rope):
    j = pl.program_id(1)

    @pl.when(j == 0)
    def _():
        h = _rms(x_ref[...]) * nw_ref[...]
        h = h * (1.0 + sc_ref[...]) + sh_ref[...]
        hb = h.astype(BF16)
        h_scr[...] = hb
        za = _dot(hb, wa_ref[...])
        cqn_ref[...] = (_rms(za[:, :q_lora]) * qaw_ref[...]).astype(BF16)
        lat = _rms(za[:, q_lora:q_lora + kv_lora]) * kvw_ref[...]
        o = q_lora + kv_lora
        kr_a = za[:, o:o + LANES]
        kr_b = za[:, o + LANES:o + 2 * LANES]
        rows_ref[:, :kv_lora] = lat
        rows_ref[:, kv_lora:] = kr_a[:, :rope]
        latb_ref[...] = lat.astype(BF16)
        krr_ref[...] = kr_a * wkr_ref[...] * cos_ref[...] + kr_b * wkrs_ref[...] * sin_ref[...]
        krsq_ref[...] = jnp.broadcast_to(jnp.sum(kr_a * kr_a, axis=-1, keepdims=True), krsq_ref.shape)

    @pl.when(j > 0)
    def _():
        zh_ref[...] = _dot(h_scr[...], wh_ref[...])


def _inproj(x, sc, sh, mod_spec, nw, wa, wh, qaw, kvw, wkr, wkrs, cos_t, sin_t, tab_map, *, tm,
            q_lora, kv_lora, rope):
    t, d = x.shape
    na = wa.shape[1]
    nh = wh.shape[1]
    tn = _pick(nh, 1024)
    nj = nh // tn
    const = lambda i, j: (0, 0)
    row = lambda i, j: (i, 0)
    kern = functools.partial(_inproj_kernel, q_lora=q_lora, kv_lora=kv_lora, rope=rope)
    return pl.pallas_call(
        kern,
        grid=(t // tm, 1 + nj),
        in_specs=[pl.BlockSpec((tm, d), row), mod_spec, mod_spec,
                  pl.BlockSpec((1, d), const),
                  pl.BlockSpec((d, na), const),
                  pl.BlockSpec((d, tn), lambda i, j: (0, jnp.maximum(j - 1, 0))),
                  pl.BlockSpec((1, q_lora), const), pl.BlockSpec((1, kv_lora), const),
                  pl.BlockSpec((1, LANES), const), pl.BlockSpec((1, LANES), const),
                  pl.BlockSpec((tm, LANES), tab_map), pl.BlockSpec((tm, LANES), tab_map)],
        out_specs=[pl.BlockSpec((tm, q_lora), row),
                   pl.BlockSpec((tm, kv_lora + rope), row),
                   pl.BlockSpec((tm, kv_lora), row),
                   pl.BlockSpec((tm, LANES), row),
                   pl.BlockSpec((tm, LANES), row),
                   pl.BlockSpec((tm, tn), lambda i, j: (i, jnp.maximum(j - 1, 0)))],
        out_shape=[jax.ShapeDtypeStruct((t, q_lora), BF16),
                   jax.ShapeDtypeStruct((t, kv_lora + rope), F32),
                   jax.ShapeDtypeStruct((t, kv_lora), BF16),
                   jax.ShapeDtypeStruct((t, LANES), F32),
                   jax.ShapeDtypeStruct((t, LANES), F32),
                   jax.ShapeDtypeStruct((t, nh), F32)],
        scratch_shapes=[pltpu.VMEM((tm, d), BF16)],
        compiler_params=_cparams("parallel", "arbitrary"),
        name="inproj",
    )(x, sc, sh, nw, wa, wh, qaw, kvw, wkr, wkrs, cos_t, sin_t)


def _q_kernel(cqn_ref, w_ref, wn_ref, wa_ref, wb_ref, c_ref, s_ref, o_ref, *, scale, d_head):
    y = _dot(cqn_ref[...], w_ref[...])
    nope = y[:, :LANES]
    a = y[:, LANES:2 * LANES]
    b = y[:, 2 * LANES:]
    ssq = jnp.sum(nope * nope, axis=-1, keepdims=True) + 0.5 * jnp.sum(a * a, axis=-1, keepdims=True)
    rinv = lax.rsqrt(ssq * (1.0 / d_head) + NORM_EPS) * scale
    o_ref[:, :LANES] = (nope * rinv * wn_ref[...]).astype(o_ref.dtype)
    rot = a * wa_ref[...] * c_ref[...] + b * wb_ref[...] * s_ref[...]
    o_ref[:, LANES:] = (rot * rinv).astype(o_ref.dtype)


def _queries(cqn, wq, wn, wa, wb, tabs, tab_map, *, tm, scale, d_head):
    t, r = cqn.shape
    nheads = wq.shape[0]
    wout = 2 * LANES
    const = lambda i, h: (0, 0)
    kern = functools.partial(_q_kernel, scale=scale, d_head=d_head)
    tab_spec = pl.BlockSpec((tm, LANES), lambda i, h: tab_map(i, h))
    return pl.pallas_call(
        kern,
        grid=(t // tm, nheads),
        in_specs=[pl.BlockSpec((tm, r), lambda i, h: (i, 0)),
                  pl.BlockSpec((None, r, 3 * LANES), lambda i, h: (h, 0, 0)),
                  pl.BlockSpec((1, LANES), const), pl.BlockSpec((1, LANES), const),
                  pl.BlockSpec((1, LANES), const),
                  tab_spec, tab_spec],
        out_specs=pl.BlockSpec((None, tm, wout), lambda i, h: (h, i, 0)),
        out_shape=jax.ShapeDtypeStruct((nheads, t, wout), BF16),
        compiler_params=_cparams("parallel", "parallel"),
        name="queries",
    )(cqn, wq, wn, wa, wb, *tabs)


def _kv_kernel(latb_ref, krr_ref, krsq_ref, w_ref, wn_ref, k_ref, v_ref, *, d_head):
    y = _dot(latb_ref[...], w_ref[...])
    kn = y[:, :LANES]
    ssq = jnp.sum(kn * kn, axis=-1, keepdims=True) + krsq_ref[:, :1]
    rinv = lax.rsqrt(ssq * (1.0 / d_head) + NORM_EPS)
    k_ref[:, :LANES] = (kn * rinv * wn_ref[...]).astype(BF16)
    k_ref[:, LANES:] = (krr_ref[...] * rinv).astype(BF16)
    v_ref[...] = y[:, LANES:].astype(BF16)


def _keys_values(latb, krr, krsq, wkv, wn, *, tm, d_head):
    t, c = latb.shape
    nheads = wkv.shape[0]
    row = lambda i, h: (i, 0)
    return pl.pallas_call(
        functools.partial(_kv_kernel, d_head=d_head),
        grid=(t // tm, nheads),
        in_specs=[pl.BlockSpec((tm, c), row), pl.BlockSpec((tm, LANES), row), pl.BlockSpec((tm, LANES), row),
                  pl.BlockSpec((None, c, 2 * LANES), lambda i, h: (h, 0, 0)),
                  pl.BlockSpec((1, LANES), lambda i, h: (0, 0))],
        out_specs=[pl.BlockSpec((None, tm, 2 * LANES), lambda i, h: (h, i, 0)),
                   pl.BlockSpec((None, tm, LANES), lambda i, h: (h, i, 0))],
        out_shape=[jax.ShapeDtypeStruct((nheads, t, 2 * LANES), BF16),
                   jax.ShapeDtypeStruct((nheads, t, LANES), BF16)],
        compiler_params=_cparams("parallel", "parallel"),
        name="keys_values",
    )(latb, krr, krsq, wkv, wn)


def _flash_kernel(q_ref, k_ref, v_ref, o_ref, m_scr, l_scr, acc_scr, *, tq):
    qi = pl.program_id(2)
    ki = pl.program_id(3)

    @pl.when(ki == 0)
    def _():
        m_scr[...] = jnp.full(m_scr.shape, NEG_INF, F32)
        l_scr[...] = jnp.zeros(l_scr.shape, F32)
        acc_scr[...] = jnp.zeros(acc_scr.shape, F32)

    hps = q_ref.shape[0]
    dv = v_ref.shape[-1]

    def block(on_diagonal):
        for hh in range(hps):
            s = _dot_nt(q_ref[hh], k_ref[hh])
            if on_diagonal:
                qpos = lax.broadcasted_iota(jnp.int32, s.shape, 0)
                kpos = lax.broadcasted_iota(jnp.int32, s.shape, 1)
                s = jnp.where(kpos <= qpos, s, NEG_INF)
            m_old = m_scr[hh]
            m_new = jnp.maximum(m_old, jnp.max(s, axis=-1, keepdims=True))
            alpha = jnp.exp(m_old - m_new)
            p = jnp.exp(s - m_new)
            l_scr[hh] = alpha * l_scr[hh] + jnp.sum(p, axis=-1, keepdims=True)
            acc_scr[hh] = alpha * acc_scr[hh] + _dot(p.astype(BF16), v_ref[hh])
            m_scr[hh] = m_new

    pl.when(ki < qi)(functools.partial(block, False))
    pl.when(ki == qi)(functools.partial(block, True))

    @pl.when(ki == pl.num_programs(3) - 1)
    def _():
        for hh in range(hps):
            o_ref[:, hh * dv:(hh + 1) * dv] = acc_scr[hh] / l_scr[hh]


def _flash(q, k, v, *, nbatch, seq, tq, hps):
    nheads, t, dq = q.shape
    dv = v.shape[-1]
    nq = seq // tq
    return pl.pallas_call(
        functools.partial(_flash_kernel, tq=tq),
        grid=(nbatch, nheads // hps, nq, nq),
        in_specs=[pl.BlockSpec((hps, tq, dq), lambda b, h, qi, ki: (h, b * nq + qi, 0)),
                  pl.BlockSpec((hps, tq, dq), lambda b, h, qi, ki: (h, b * nq + jnp.minimum(ki, qi), 0)),
                  pl.BlockSpec((hps, tq, dv), lambda b, h, qi, ki: (h, b * nq + jnp.minimum(ki, qi), 0))],
        out_specs=pl.BlockSpec((tq, hps * dv), lambda b, h, qi, ki: (b * nq + qi, h)),
        out_shape=jax.ShapeDtypeStruct((t, nheads * dv), F32),
        scratch_shapes=[pltpu.VMEM((hps, tq, 1), F32), pltpu.VMEM((hps, tq, 1), F32),
                        pltpu.VMEM((hps, tq, dv), F32)],
        compiler_params=_cparams("parallel", "parallel", "parallel", "arbitrary"),
        name="prompt_attention",
    )(q, k, v)


def _absorb_kernel(q_ref, wuk_ref, wkn_ref, wkr_ref, o_ref, *, kv_lora):
    q = q_ref[...].astype(F32)
    qn = (q[:, :LANES] * wkn_ref[...]).astype(BF16)
    o_ref[:, :kv_lora] = _dot_nt(qn, wuk_ref[...]).astype(BF16)
    o_ref[:, kv_lora:] = (q[:, LANES:] * wkr_ref[...]).astype(BF16)


def _absorb(q, wuk, wkn, wkr, *, kv_lora):
    nheads, t, wq = q.shape
    wout = kv_lora + LANES
    const = lambda h: (0, 0)
    return pl.pallas_call(
        functools.partial(_absorb_kernel, kv_lora=kv_lora),
        grid=(nheads,),
        in_specs=[pl.BlockSpec((None, t, wq), lambda h: (h, 0, 0)),
                  pl.BlockSpec((None, kv_lora, LANES), lambda h: (h, 0, 0)),
                  pl.BlockSpec((1, LANES), const), pl.BlockSpec((1, LANES), const)],
        out_specs=pl.BlockSpec((t, wout), lambda h: (0, h)),
        out_shape=jax.ShapeDtypeStruct((t, nheads * wout), BF16),
        compiler_params=_cparams("parallel"),
        name="absorb_queries",
    )(q, wuk, wkn, wkr)


def _sattn_kernel(pt_ref, *refs, n_group, kv_lora, rope, nheads, ntok, d_head, page):
    page_refs = refs[:n_group]
    (new_ref, qf_ref, wukt_ref, cs_ref, csn_ref, o_ref,
     lhs_scr, latb_scr, m_scr, l_scr, acc_scr) = refs[n_group:]
    step = pl.program_id(1)
    nrow = nheads * ntok
    nk_w = wukt_ref.shape[0]

    @pl.when(step == 0)
    def _():
        lhs_scr[:nk_w, :] = wukt_ref[...]
        lhs_scr[nk_w:, :] = qf_ref[:, :kv_lora]
        m_scr[...] = jnp.full(m_scr.shape, NEG_INF, F32)
        l_scr[...] = jnp.zeros(l_scr.shape, F32)
        acc_scr[...] = jnp.zeros(acc_scr.shape, F32)

    eye = (lax.broadcasted_iota(jnp.int32, (nrow, nrow), 0)
           == lax.broadcasted_iota(jnp.int32, (nrow, nrow), 1))

    def as_row(col):
        return jnp.sum(jnp.where(eye, col, 0.0), axis=0, keepdims=True)

    def scores(lat_b, kr_t, cs):
        nk = lat_b.shape[1]
        r = _dot(lhs_scr[...], lat_b)
        kn = r[:nk_w]
        kn2 = jnp.sum((kn * kn).reshape(nheads, nk_w // nheads, nk), axis=1)
        krsq = jnp.sum(kr_t * kr_t, axis=0, keepdims=True)
        rinv = lax.rsqrt((kn2 + krsq) * (1.0 / d_head) + NORM_EPS)
        rinv_rows = jnp.concatenate([rinv] * ntok, axis=0)
        f2 = (jnp.concatenate([kr_t, kr_t], axis=0) * cs).astype(BF16)
        return (r[nk_w:] + _dot(qf_ref[:, kv_lora:], f2)) * rinv_rows

    def update(s, lat_b):
        m_old = m_scr[...]
        m_new = jnp.maximum(m_old, jnp.max(s, axis=-1, keepdims=True))
        alpha = jnp.exp(m_old - m_new)
        p = jnp.exp(s - m_new)
        l_scr[...] = alpha * l_scr[...] + jnp.sum(p, axis=-1, keepdims=True)
        acc_scr[...] = acc_scr[...] * as_row(alpha) + _dot_nt(lat_b, p.astype(BF16))
        m_scr[...] = m_new

    per = max(1, min(n_group, MXU_WIDTH // page))
    s_parts = []
    for g in range(0, n_group, per):
        cols = slice(g * page, (g + per) * page)
        lat_b = jnp.concatenate([page_refs[g + u][:kv_lora, :] for u in range(per)], axis=1).astype(BF16)
        kr_t = jnp.concatenate([page_refs[g + u][kv_lora:, :] for u in range(per)], axis=1)
        latb_scr[:, cols] = lat_b
        s_parts.append(scores(lat_b, kr_t, cs_ref[:, cols]))
    update(jnp.concatenate(s_parts, axis=1), latb_scr[...])

    @pl.when(step == pl.num_programs(1) - 1)
    def _():
        key = lax.broadcasted_iota(jnp.int32, (nrow, page), 1)
        tok = lax.broadcasted_iota(jnp.int32, (nrow, page), 0) // nheads
        lat_b = new_ref[:kv_lora, :].astype(BF16)
        s_new = scores(lat_b, new_ref[kv_lora:, :], csn_ref[...])
        update(jnp.where(key <= tok, s_new, NEG_INF), lat_b)
        o_ref[...] = acc_scr[...] / as_row(l_scr[...])


def _sample_attention(page_table, cache_t, new_pages_t, qf, wukt, cs_pages, cs_new, *,
                      n_group, kv_lora, rope, nheads, ntok, d_head):
    nb, n_pages = page_table.shape
    cw, page = cache_t.shape[2], cache_t.shape[3]
    nrow = nheads * ntok
    wqf = qf.shape[-1]
    nsteps = n_pages // n_group

    def page_map(g):
        return lambda b, s, pt: (0, pt[b * n_pages + s * n_group + g], 0, 0)

    kern = functools.partial(_sattn_kernel, n_group=n_group, kv_lora=kv_lora, rope=rope, nheads=nheads,
                             ntok=ntok, d_head=d_head, page=page)
    in_specs = [pl.BlockSpec((None, None, cw, page), page_map(g)) for g in range(n_group)]
    in_specs += [pl.BlockSpec((None, cw, page), lambda b, s, pt: (b, 0, 0)),
                 pl.BlockSpec((None, nrow, wqf), lambda b, s, pt: (b, 0, 0)),
                 pl.BlockSpec(wukt.shape, lambda b, s, pt: (0, 0)),
                 pl.BlockSpec((2 * rope, n_group * page), lambda b, s, pt: (0, s)),
                 pl.BlockSpec((2 * rope, page), lambda b, s, pt: (0, 0))]
    grid_spec = pltpu.PrefetchScalarGridSpec(
        num_scalar_prefetch=1,
        grid=(nb, nsteps),
        in_specs=in_specs,
        out_specs=pl.BlockSpec((None, kv_lora, nrow), lambda b, s, pt: (b, 0, 0)),
        scratch_shapes=[pltpu.VMEM((wukt.shape[0] + nrow, kv_lora), BF16),
                        pltpu.VMEM((kv_lora, n_group * page), BF16),
                        pltpu.VMEM((nrow, 1), F32), pltpu.VMEM((nrow, 1), F32),
                        pltpu.VMEM((kv_lora, nrow), F32)])
    return pl.pallas_call(
        kern,
        grid_spec=grid_spec,
        out_shape=jax.ShapeDtypeStruct((nb, kv_lora, nrow), F32),
        compiler_params=_cparams("parallel", "arbitrary"),
        name="sample_attention",
    )(page_table.reshape(-1), *([cache_t] * n_group), new_pages_t, qf, wukt, cs_pages, cs_new)


def _uv_kernel(lm_ref, w_ref, o_ref):
    o_ref[...] = _dot(lm_ref[...].astype(BF16), w_ref[...])


def _uv_project(lm, wuv):
    t = lm.shape[0]
    nheads, c, dv = wuv.shape
    return pl.pallas_call(
        _uv_kernel,
        grid=(nheads,),
        in_specs=[pl.BlockSpec((t, c), lambda h: (0, h)),
                  pl.BlockSpec((None, c, dv), lambda h: (h, 0, 0))],
        out_specs=pl.BlockSpec((t, dv), lambda h: (0, h)),
        out_shape=jax.ShapeDtypeStruct((t, nheads * dv), F32),
        compiler_params=_cparams("parallel"),
        name="value_up_projection",
    )(lm, wuv)


def _cumsum_rows(g, chunk):
    if chunk <= SUBLANES:
        ridx = lax.broadcasted_iota(jnp.int32, g.shape, 0)
        out = jnp.zeros_like(g)
        for s in range(chunk):
            out = out + jnp.where(ridx >= s, g[s:s + 1, :], 0.0)
        return out
    rows = lax.broadcasted_iota(jnp.int32, (chunk, chunk), 0)
    cols = lax.broadcasted_iota(jnp.int32, (chunk, chunk), 1)
    tri = (cols <= rows).astype(BF16)
    hi = g.astype(BF16)
    r1 = g - hi.astype(F32)
    mid = r1.astype(BF16)
    lo = (r1 - mid.astype(F32)).astype(BF16)
    return _dot(tri, hi) + (_dot(tri, mid) + _dot(tri, lo))


def _hgrn_kernel(hq_ref, hf_ref, hi_ref, hg_ref, lbp_ref, s0_ref, gnw_ref, o_ref, sout_ref, st_scr, *,
                 chunk, sub, nchunk, t_valid, has_s0, hps):
    tb = pl.program_id(2)
    last = tb == pl.num_programs(2) - 1
    gnw = gnw_ref[...]

    @pl.when(tb == 0)
    def _():
        for hh in range(hps):
            if has_s0:
                st_scr[hh] = s0_ref[hh].T
            else:
                st_scr[hh] = jnp.zeros(st_scr.shape[1:], F32)

    finals = []
    for hh in range(hps):
        cs = slice(hh * LANES, (hh + 1) * LANES)

        a = lbp_ref[:, cs]
        e = jnp.exp(a - jnp.max(a, axis=0, keepdims=True))
        lb = e[0:1, :] / jnp.sum(e, axis=0, keepdims=True)

        st = st_scr[hh]
        for c in range(nchunk):
            sl = slice(c * chunk, (c + 1) * chunk)
            zq = hq_ref[sl, cs]
            q = zq * _sigmoid(zq)
            k = (1.0 - lb) * _sigmoid(-hf_ref[sl, cs])
            g = jnp.log(1.0 - k)
            if t_valid is not None:
                valid = (tb * (nchunk * chunk) + c * chunk
                         + lax.broadcasted_iota(jnp.int32, k.shape, 0)) < t_valid
                k = jnp.where(valid, k, 0.0)
                g = jnp.where(valid, g, 0.0)
            v = hi_ref[sl, cs]
            zg = hg_ref[sl, cs]
            gate = zg * _sigmoid(zg)

            gc = _cumsum_rows(g, chunk)
            o = _dot_nt(q * jnp.exp(gc), st)
            g_last = gc[chunk - 1:chunk, :]
            u_t = _dot_tn(v, k * jnp.exp(g_last - gc))

            parts = []
            for i in range(chunk // sub):
                r0 = i * sub
                gi = gc[r0:r0 + sub, :]
                qi = q[r0:r0 + sub, :]
                if i > 0:
                    ref_row = gc[r0 - 1:r0, :]
                    a_off = _dot_nt(qi * jnp.exp(gi - ref_row), k[:r0, :] * jnp.exp(ref_row - gc[:r0, :]))
                    o_i = _dot(a_off, v[:r0, :])
                else:
                    o_i = jnp.zeros((sub, v.shape[1]), F32)
                trow = lax.broadcasted_iota(jnp.int32, (sub, 1), 0)
                for s in range(sub):
                    keep = trow >= s
                    d = jnp.where(keep, gi - gc[r0 + s:r0 + s + 1, :], 0.0)
                    w = jnp.exp(d) * qi * k[r0 + s:r0 + s + 1, :]
                    a_col = jnp.where(keep, jnp.sum(w, axis=-1, keepdims=True), 0.0)
                    o_i = o_i + a_col * v[r0 + s:r0 + s + 1, :]
                parts.append(o_i)
            o = o + (parts[0] if len(parts) == 1 else jnp.concatenate(parts, axis=0))
            st = st * jnp.exp(g_last) + u_t
            o_ref[sl, cs] = _rms(o) * gnw * gate

        st_scr[hh] = st
        finals.append(st)

    @pl.when(last)
    def _():
        for hh in range(hps):
            sout_ref[hh] = finals[hh].T


def _hgrn(zh, lbp, s0, gnw, *, nheads, hps, tblock, chunk, sub, t_valid):
    n, t, _ = zh.shape
    dk = LANES
    has_s0 = s0 is not None
    nlb = lbp.shape[0]
    ngrp = nheads // hps
    if not has_s0:
        s0 = jnp.zeros((1, hps, dk, dk), F32)
        s0_spec = pl.BlockSpec((None, hps, dk, dk), lambda b, h, tb: (0, 0, 0, 0))
    else:
        s0_spec = pl.BlockSpec((None, hps, dk, dk), lambda b, h, tb: (b, h, 0, 0))

    def col(group):
        return pl.BlockSpec((None, tblock, hps * dk), lambda b, h, tb: (b, tb, group * ngrp + h))

    kern = functools.partial(_hgrn_kernel, chunk=chunk, sub=sub, nchunk=tblock // chunk, t_valid=t_valid,
                             has_s0=has_s0, hps=hps)
    return pl.pallas_call(
        kern,
        grid=(n, ngrp, t // tblock),
        in_specs=[col(0), col(1), col(2), col(3),
                  pl.BlockSpec((nlb, hps * dk), lambda b, h, tb: (0, h)),
                  s0_spec,
                  pl.BlockSpec((1, dk), lambda b, h, tb: (0, 0))],
        out_specs=[pl.BlockSpec((None, tblock, hps * dk), lambda b, h, tb: (b, tb, h)),
                   pl.BlockSpec((None, hps, dk, dk), lambda b, h, tb: (b, h, 0, 0))],
        out_shape=[jax.ShapeDtypeStruct((n, t, nheads * dk), F32),
                   jax.ShapeDtypeStruct((n, nheads, dk, dk), F32)],
        scratch_shapes=[pltpu.VMEM((hps, dk, dk), F32)],
        compiler_params=_cparams("parallel", "parallel", "arbitrary"),
        name="hgrn2",
    )(zh, zh, zh, zh, lbp, s0, gnw)


def _outproj_kernel(oa_ref, orec_ref, x_ref, ga_ref, scf_ref, shf_ref, gnw_ref, fnw_ref, wo_ref, rw_ref,
                    rb_ref, x1_ref, h2_ref, topi_ref, gate_ref, *, n_attn):
    oan = (_rms(oa_ref[...]) * gnw_ref[...]).astype(BF16)
    mix = _dot(oan, wo_ref[:n_attn, :]) + _dot(orec_ref[...].astype(BF16), wo_ref[n_attn:, :])
    x1 = x_ref[...] + ga_ref[...] * mix
    x1_ref[...] = x1
    h2 = _rms(x1) * fnw_ref[...] * (1.0 + scf_ref[...]) + shf_ref[...]
    h2_ref[...] = h2
    logits = jnp.dot(h2, rw_ref[...], preferred_element_type=F32,
                     precision=lax.Precision.HIGHEST) + rb_ref[...]
    lane = lax.broadcasted_iota(jnp.int32, logits.shape, 1).astype(F32)
    vals = []
    topi = jnp.zeros(logits.shape, F32)
    work = logits
    for kk in range(TOP_K):
        m = jnp.max(work, axis=-1, keepdims=True)
        idx = jnp.min(jnp.where(work == m, lane, float(LANES)), axis=-1, keepdims=True)
        vals.append(m)
        topi = jnp.where(lane == float(kk), idx, topi)
        work = jnp.where(lane == idx, -3e38, work)
    es = [jnp.exp(vv - vals[0]) for vv in vals]
    den = es[0]
    for ee in es[1:]:
        den = den + ee
    gates = jnp.zeros(logits.shape, F32)
    for kk in range(TOP_K):
        gates = jnp.where(lane == float(kk), es[kk] / den, gates)
    topi_ref[...] = topi.astype(jnp.int32)
    gate_ref[...] = gates


def _outproj(oa, orec, x, ga, scf, shf, mod_spec, gnw, fnw, wo, rw, rb, *, tm):
    t, d = x.shape
    n_attn = oa.shape[1]
    n_rec = orec.shape[1]
    row = lambda i: (i, 0)
    const = lambda i: (0, 0)
    return pl.pallas_call(
        functools.partial(_outproj_kernel, n_attn=n_attn),
        grid=(t // tm,),
        in_specs=[pl.BlockSpec((tm, n_attn), row), pl.BlockSpec((tm, n_rec), row), pl.BlockSpec((tm, d), row),
                  mod_spec, mod_spec, mod_spec,
                  pl.BlockSpec((1, n_attn), const), pl.BlockSpec((1, d), const),
                  pl.BlockSpec(wo.shape, const), pl.BlockSpec(rw.shape, const), pl.BlockSpec((1, LANES), const)],
        out_specs=[pl.BlockSpec((tm, d), row), pl.BlockSpec((tm, d), row),
                   pl.BlockSpec((tm, LANES), row), pl.BlockSpec((tm, LANES), row)],
        out_shape=[jax.ShapeDtypeStruct((t, d), F32), jax.ShapeDtypeStruct((t, d), F32),
                   jax.ShapeDtypeStruct((t, LANES), jnp.int32), jax.ShapeDtypeStruct((t, LANES), F32)],
        compiler_params=_cparams("parallel"),
        name="outproj_router",
    )(oa, orec, x, ga, scf, shf, gnw, fnw, wo, rw, rb)


def _gather_kernel(idx_ref, src_ref, dst_ref, sem, *, rows_per_step):
    def row_copy(r, src_row):
        return pltpu.make_async_copy(src_ref.at[src_row], dst_ref.at[r], sem)

    def issue(r, carry):
        row_copy(r, idx_ref[r]).start()
        return carry

    def drain(r, carry):
        row_copy(r, 0).wait()
        return carry

    lax.fori_loop(0, rows_per_step, issue, 0)
    lax.fori_loop(0, rows_per_step, drain, 0)


def _gather_rows(idx, src, *, rows_per_step):
    n = idx.shape[0]
    return pl.pallas_call(
        functools.partial(_gather_kernel, rows_per_step=rows_per_step),
        grid=(n // rows_per_step,),
        in_specs=[pl.BlockSpec((rows_per_step,), lambda i: (i,), memory_space=pltpu.SMEM),
                  pl.BlockSpec(memory_space=pl.ANY)],
        out_specs=pl.BlockSpec((rows_per_step,) + src.shape[1:], lambda i: (i, 0, 0)),
        out_shape=jax.ShapeDtypeStruct((n,) + src.shape[1:], src.dtype),
        scratch_shapes=[pltpu.SemaphoreType.DMA(())],
        compiler_params=_cparams("parallel"),
        name="gather_rows",
    )(idx, src)


def _moe_kernel(te_ref, nu_ref, idx0_ref, idxn_ref, h_ref, w1g_ref, w1l_ref, b1g_ref, b1l_ref, w2_ref, b2_ref,
                o_ref, xbuf, xb_scr, acc_scr, sem, *, nseg, nj):
    i = pl.program_id(0)
    j = pl.program_id(1)
    tm = xb_scr.shape[0]
    npk = nseg // 2
    half = npk * LANES
    per = tm // nj
    nu = nu_ref[0]

    def row_copy(slot, r, src_row):
        dst = xbuf.at[pl.ds(pl.multiple_of((slot * tm + r) * npk, npk), npk), :]
        return pltpu.make_async_copy(h_ref.at[src_row], dst, sem.at[slot])

    def issue_rows(slot, idx_ref, start, count):
        def body(r, carry):
            row_copy(slot, r, idx_ref[r]).start()
            return carry
        lax.fori_loop(start, start + count, body, 0)

    def wait_rows(slot):
        def body(r, carry):
            row_copy(slot, r, 0).wait()
            return carry
        lax.fori_loop(0, tm, body, 0)

    @pl.when(jnp.logical_and(i == 0, j == 0))
    def _():
        issue_rows(0, idx0_ref, 0, tm)

    for slot in (0, 1):
        @pl.when(jnp.logical_and(i % 2 == slot, i < nu))
        def _():
            @pl.when(j == 0)
            def _():
                wait_rows(slot)
                for s in range(npk):
                    u = xbuf[pl.ds(slot * tm * npk + s, tm, stride=npk), :]
                    lo = lax.bitcast_convert_type(u << 16, F32)
                    hi = lax.bitcast_convert_type(u & jnp.uint32(0xFFFF0000), F32)
                    xb_scr[:, s * LANES:(s + 1) * LANES] = lo.astype(BF16)
                    xb_scr[:, half + s * LANES:half + (s + 1) * LANES] = hi.astype(BF16)
                acc_scr[...] = jnp.broadcast_to(b2_ref[...], acc_scr.shape)

    @pl.when(i < nu)
    def _():
        nxt = (i + 1) % 2
        for u in range(per):
            r = j * per + u
            row_copy(nxt, r, idxn_ref[r]).start()

        xb = xb_scr[...]
        yg = _dot(xb, w1g_ref[...].astype(BF16)) + b1g_ref[...]
        yl = _dot(xb, w1l_ref[...].astype(BF16)) + b1l_ref[...]
        glu = jnp.minimum(yg, SWIGLU_LIMIT)
        lin = jnp.clip(yl, -SWIGLU_LIMIT, SWIGLU_LIMIT)
        a = (glu * _sigmoid(SWIGLU_ALPHA * glu) * (lin + 1.0)).astype(BF16)
        acc_scr[...] += _dot(a, w2_ref[...].astype(BF16))

        @pl.when(j == nj - 1)
        def _():
            for s in range(nseg):
                o_ref[pl.ds(s, tm, stride=nseg), :] = acc_scr[:, s * LANES:(s + 1) * LANES]

        @pl.when(jnp.logical_and(i == nu - 1, j == nj - 1))
        def _():
            wait_rows(nxt)

    @pl.when(jnp.logical_and(i >= nu, j == 0))
    def _():
        o_ref[...] = jnp.zeros(o_ref.shape, F32)


def _moe_ffn(tile_e, n_used, slot_tok, h2p, w1, b1, w2, b2, *, tm, th, nseg):
    d = nseg * LANES
    p = slot_tok.shape[0]
    ne, _, two_de = w1.shape
    de = two_de // 2
    nj = de // th
    ntiles = p // tm
    assert tm % nj == 0 and nseg % 2 == 0

    def tile(i, nu):
        return jnp.minimum(i, nu[0] - 1)

    def hid(i, j, nu):
        return jnp.where(i < nu[0], j, nj - 1)

    grid_spec = pltpu.PrefetchScalarGridSpec(
        num_scalar_prefetch=2,
        grid=(ntiles, nj),
        in_specs=[pl.BlockSpec((tm,), lambda i, j, te, nu: (0,), memory_space=pltpu.SMEM),
                  pl.BlockSpec((tm,), lambda i, j, te, nu: (jnp.minimum(i + 1, ntiles - 1),),
                               memory_space=pltpu.SMEM),
                  pl.BlockSpec(memory_space=pl.ANY),
                  pl.BlockSpec((None, d, th), lambda i, j, te, nu: (te[tile(i, nu)], 0, hid(i, j, nu))),
                  pl.BlockSpec((None, d, th), lambda i, j, te, nu: (te[tile(i, nu)], 0, nj + hid(i, j, nu))),
                  pl.BlockSpec((None, 1, th), lambda i, j, te, nu: (te[tile(i, nu)], 0, hid(i, j, nu))),
                  pl.BlockSpec((None, 1, th), lambda i, j, te, nu: (te[tile(i, nu)], 0, nj + hid(i, j, nu))),
                  pl.BlockSpec((None, th, d), lambda i, j, te, nu: (te[tile(i, nu)], hid(i, j, nu), 0)),
                  pl.BlockSpec((None, 1, d), lambda i, j, te, nu: (te[tile(i, nu)], 0, 0))],
        out_specs=pl.BlockSpec((tm * nseg, LANES), lambda i, j, te, nu: (i, 0)),
        scratch_shapes=[pltpu.VMEM((2 * tm * (nseg // 2), LANES), jnp.uint32),
                        pltpu.VMEM((tm, d), BF16),
                        pltpu.VMEM((tm, d), F32),
                        pltpu.SemaphoreType.DMA((2,))])
    return pl.pallas_call(
        functools.partial(_moe_kernel, nseg=nseg, nj=nj),
        grid_spec=grid_spec,
        out_shape=jax.ShapeDtypeStruct((p * nseg, LANES), F32),
        compiler_params=_cparams("arbitrary", "arbitrary"),
        name="moe_experts",
    )(tile_e, n_used, slot_tok, slot_tok, h2p, w1, w1, b1.reshape(ne, 1, two_de),
      b1.reshape(ne, 1, two_de), w2, b2.reshape(ne, 1, d))


def _combine_kernel(*refs, nseg):
    y_refs = refs[:TOP_K]
    gate_ref, x1_ref, gf_ref, o_ref = refs[TOP_K:]
    tm = x1_ref.shape[0]
    gates = gate_ref[...]
    gk = [gates[:, kk:kk + 1] for kk in range(TOP_K)]
    for s in range(nseg):
        seg = slice(s * LANES, (s + 1) * LANES)
        acc = gk[0] * y_refs[0][pl.ds(s, tm, stride=nseg), :]
        for kk in range(1, TOP_K):
            acc = acc + gk[kk] * y_refs[kk][pl.ds(s, tm, stride=nseg), :]
        o_ref[:, seg] = x1_ref[:, seg] + gf_ref[:, seg] * acc


def _combine(yg2, gates, x1, gf, mod_spec, *, tm, row_off, nblk_all, nseg):
    t, d = x1.shape

    def ymap(kk):
        return lambda i: (kk * nblk_all + row_off + i, 0)

    return pl.pallas_call(
        functools.partial(_combine_kernel, nseg=nseg),
        grid=(t // tm,),
        in_specs=[pl.BlockSpec((tm * nseg, LANES), ymap(kk)) for kk in range(TOP_K)] + [
            pl.BlockSpec((tm, LANES), lambda i: (i + row_off, 0)),
            pl.BlockSpec((tm, d), lambda i: (i, 0)),
            mod_spec],
        out_specs=pl.BlockSpec((tm, d), lambda i: (i, 0)),
        out_shape=jax.ShapeDtypeStruct((t, d), F32),
        compiler_params=_cparams("parallel"),
        name="moe_combine",
    )(*([yg2] * TOP_K), gates, x1, gf)


def _rope_angles(pos, half):
    inv = ROPE_THETA ** (-jnp.arange(half, dtype=F32) / half)
    ang = pos.astype(F32)[:, None] * inv[None, :]
    return jnp.cos(ang), jnp.sin(ang)


def _pad_lanes(v, width=LANES):
    v = v.reshape(1, -1).astype(F32)
    return jnp.pad(v, ((0, 0), (0, width - v.shape[1])))


def _swap_halves(v):
    h = v.shape[-1] // 2
    return jnp.concatenate([v[..., h:], v[..., :h]], -1)


def kernel(x_prompt, x_sample, cache_mla, state_hgrn, page_table, c_prompt, c_sample, w_ada, b_ada,
           attn_norm_w, w_in, q_a_norm_w, w_uq, kv_a_norm_w, w_ukv, q_head_norm_w, k_head_norm_w,
           attn_group_norm_w, hg_lower_bound, hg_norm_w, w_out, ffn_norm_w, router_w, router_b,
           w1, b1, w2, b2):
    nb_p, seq, d = x_prompt.shape
    nb_s, ntok, _ = x_sample.shape
    depth = w_ada.shape[0]
    assert depth == 1, "single-layer step"
    page = cache_mla.shape[2]
    cache_w = cache_mla.shape[3]
    n_pages = page_table.shape[1]
    past_len = n_pages * page
    q_lora, nheads, d_head = w_uq.shape[1], w_uq.shape[2], w_uq.shape[3]
    kv_lora = w_ukv.shape[1]
    rope = cache_w - kv_lora
    nope = d_head - rope
    v_head = w_ukv.shape[3] - nope
    hg_heads, hg_dk, hg_dv = state_hgrn.shape[2], state_hgrn.shape[3], state_hgrn.shape[4]
    n_experts = router_w.shape[2]
    assert nope == LANES and v_head == LANES and hg_dk == LANES and hg_dv == LANES
    assert 2 * rope == LANES and n_experts <= LANES and d % LANES == 0
    half = rope // 2
    nseg = d // LANES
    attn_scale = d_head ** -0.5
    t_p = nb_p * seq
    t_s = nb_s * ntok

    mod = _adaln(jnp.concatenate([c_prompt, c_sample], 0), w_ada[0], b_ada[0])
    mods = [mod[:, i * d:(i + 1) * d] for i in range(6)]
    mods_p = [m[:nb_p].reshape(nb_p, 1, d) for m in mods]
    mods_s = [jnp.repeat(m[nb_p:], ntok, axis=0) for m in mods]

    wi = w_in[0]
    o_kr = q_lora + kv_lora
    w_kr = wi[:, o_kr:o_kr + rope]
    zpad = jnp.zeros((d, LANES - rope), F32)
    wa = jnp.concatenate([wi[:, :o_kr], w_kr, zpad, _swap_halves(w_kr), zpad], 1).astype(BF16)
    wh = wi[:, o_kr + rope:].astype(BF16)
    wq_full = jnp.transpose(w_uq[0], (1, 0, 2))
    wq_r = wq_full[:, :, nope:]
    wq_rs = _swap_halves(wq_r)
    wq = jnp.concatenate([wq_full[:, :, :nope], wq_r, wq_rs, wq_rs, wq_r], -1).astype(BF16)
    wkv = jnp.transpose(w_ukv[0], (1, 0, 2)).astype(BF16)
    wuk = wkv[:, :, :nope]
    wuv = wkv[:, :, nope:]
    wukt = jnp.transpose(wuk, (0, 2, 1)).reshape(nheads * nope, kv_lora)
    qn_w, kn_w = q_head_norm_w[0], k_head_norm_w[0]
    qr_w, kr_w = qn_w[nope:], kn_w[nope:]
    qw_n = _pad_lanes(qn_w[:nope])
    qw_a = jnp.concatenate([qr_w, _swap_halves(qr_w)]).reshape(1, LANES)
    qw_b = jnp.concatenate([_swap_halves(qr_w), qr_w]).reshape(1, LANES)
    kw_n, kw_a, kw_b = _pad_lanes(kn_w[:nope]), _pad_lanes(kr_w), _pad_lanes(_swap_halves(kr_w))
    kw_rr = jnp.concatenate([kr_w, kr_w]).reshape(1, LANES)
    nw = attn_norm_w[0].reshape(1, d)
    qaw = q_a_norm_w[0].reshape(1, q_lora)
    kvw = kv_a_norm_w[0].reshape(1, kv_lora)

    def token_tables(pos):
        cos, sin = _rope_angles(pos, half)
        z = jnp.zeros((pos.shape[0], LANES - rope), F32)
        k_cos = jnp.concatenate([cos, cos, z], -1)
        k_sin = jnp.concatenate([-sin, sin, z], -1)
        q_cos = jnp.concatenate([cos, cos, cos, -cos], -1)
        q_sin = jnp.concatenate([-sin, sin, sin, sin], -1)
        return k_cos, k_sin, q_cos, q_sin

    tabs_p = token_tables(jnp.arange(seq))
    tabs_s = token_tables(past_len + (jnp.arange(t_s) % ntok))

    tm_p = min(512, seq)
    tm_s = t_s
    nblk = seq // tm_p
    modspec_p2 = pl.BlockSpec((None, 1, d), lambda i, j: (i // nblk, 0, 0))
    modspec_s2 = pl.BlockSpec((tm_s, d), lambda i, j: (i, 0))
    dims = dict(q_lora=q_lora, kv_lora=kv_lora, rope=rope)

    xp = x_prompt.reshape(t_p, d)
    xs = x_sample.reshape(t_s, d)
    cqn_p, rows_p, latb_p, krr_p, krsq_p, zh_p = _inproj(
        xp, mods_p[1], mods_p[0], modspec_p2, nw, wa, wh, qaw, kvw, kw_a, kw_b, tabs_p[0], tabs_p[1],
        lambda i, j: (i % nblk, 0), tm=tm_p, **dims)
    cqn_s, rows_s, latb_s, krr_s, krsq_s, zh_s = _inproj(
        xs, mods_s[1], mods_s[0], modspec_s2, nw, wa, wh, qaw, kvw, kw_a, kw_b, tabs_s[0], tabs_s[1],
        lambda i, j: (i, 0), tm=tm_s, **dims)

    q_p = _queries(cqn_p, wq, qw_n, qw_a, qw_b, tabs_p[2:], lambda i, h: (i % nblk, 0), tm=tm_p,
                   scale=attn_scale, d_head=d_head)
    k_p, v_p = _keys_values(latb_p, krr_p, krsq_p, wkv, kw_n, tm=tm_p, d_head=d_head)
    oa_p = _flash(q_p, k_p, v_p, nbatch=nb_p, seq=seq, tq=tm_p, hps=2 if nheads % 2 == 0 else 1)

    q_s = _queries(cqn_s, wq, qw_n, qw_a, qw_b, tabs_s[2:], lambda i, h: (i, 0), tm=tm_s,
                   scale=attn_scale, d_head=d_head)
    qf = _absorb(q_s, wuk, kw_n, kw_rr, kv_lora=kv_lora)
    qf = qf.reshape(nb_s, ntok * nheads, kv_lora + LANES)

    def cs_table(pos):
        cos, sin = _rope_angles(pos, half)
        return jnp.concatenate([cos, cos, sin, sin], -1).T

    cs_pages = cs_table(jnp.arange(past_len))
    cs_new = cs_table(past_len + jnp.arange(page))
    cache_t = jnp.swapaxes(cache_mla, 2, 3)
    new_pages = jnp.zeros((nb_s, page, cache_w), F32).at[:, :ntok].set(rows_s.reshape(nb_s, ntok, cache_w))
    new_pages_t = jnp.swapaxes(new_pages, 1, 2)
    n_group = 8
    while n_pages % n_group:
        n_group //= 2
    lat_mix = _sample_attention(page_table, cache_t, new_pages_t, qf, wukt, cs_pages, cs_new,
                                n_group=n_group, kv_lora=kv_lora, rope=rope, nheads=nheads, ntok=ntok,
                                d_head=d_head)
    lat_mix = jnp.swapaxes(lat_mix, 1, 2)
    oa_s = _uv_project(lat_mix.reshape(t_s, nheads * kv_lora), wuv)

    gnw_h = hg_norm_w[0].reshape(1, hg_dv)
    chunk_p = min(HG_CHUNK, seq)
    orec_p, st_p = _hgrn(zh_p.reshape(nb_p, seq, -1), hg_lower_bound, None, gnw_h, nheads=hg_heads,
                         hps=2 if hg_heads % 2 == 0 else 1,
                         tblock=min(256, seq), chunk=chunk_p, sub=min(HG_SUB, chunk_p), t_valid=None)
    t_pad = -(-ntok // SUBLANES) * SUBLANES
    zh_s3 = jnp.pad(zh_s.reshape(nb_s, ntok, -1), ((0, 0), (0, t_pad - ntok), (0, 0)))
    orec_s, st_s = _hgrn(zh_s3, hg_lower_bound, state_hgrn[0], gnw_h, nheads=hg_heads, hps=hg_heads,
                         tblock=t_pad, chunk=t_pad, sub=t_pad, t_valid=ntok)
    orec_s = orec_s[:, :ntok].reshape(t_s, -1)

    wo = w_out[0].astype(BF16)
    rw = jnp.pad(router_w[0], ((0, 0), (0, LANES - n_experts)))
    rb = jnp.concatenate([router_b[0], jnp.full((LANES - n_experts,), NEG_INF, F32)]).reshape(1, LANES)
    gnw_a = attn_group_norm_w[0].reshape(1, -1)
    fnw = ffn_norm_w[0].reshape(1, d)
    modspec_p1 = pl.BlockSpec((None, 1, d), lambda i: (i // nblk, 0, 0))
    modspec_s1 = pl.BlockSpec((tm_s, d), lambda i: (i, 0))
    x1_p, h2_p, ti_p, gt_p = _outproj(oa_p, orec_p.reshape(t_p, -1), xp, mods_p[2], mods_p[4], mods_p[3],
                                      modspec_p1, gnw_a, fnw, wo, rw, rb, tm=tm_p)
    x1_s, h2_s, ti_s, gt_s = _outproj(oa_s, orec_s, xs, mods_s[2], mods_s[4], mods_s[3],
                                      modspec_s1, gnw_a, fnw, wo, rw, rb, tm=tm_s)

    n_tok = t_p + t_s
    h2b = jnp.concatenate([h2_p, h2_s], 0).astype(BF16)
    h2w = lax.bitcast_convert_type(h2b, jnp.uint16).astype(jnp.uint32)
    h2p = (h2w[:, :d // 2] | (h2w[:, d // 2:] << 16)).reshape(n_tok, nseg // 2, LANES)
    top_i = jnp.concatenate([ti_p[:, :TOP_K], ti_s[:, :TOP_K]], 0)
    gates = jnp.concatenate([gt_p, gt_s], 0)
    rows_per_step = 1024
    tm_e = rows_per_step if n_tok >= rows_per_step else 1 << (n_tok.bit_length() - 1)
    n_assign = n_tok * TOP_K
    flat_e = top_i.reshape(-1)
    onehot = (flat_e[:, None] == jnp.arange(n_experts, dtype=jnp.int32)[None, :]).astype(jnp.int32)
    csum = jnp.cumsum(onehot, axis=0)
    counts = csum[-1]
    rank = jnp.take_along_axis(csum, flat_e[:, None], axis=1)[:, 0] - 1
    padded = (counts + tm_e - 1) // tm_e * tm_e
    pend = jnp.cumsum(padded)
    pstart = pend - padded
    dest = (pstart[flat_e] + rank).astype(jnp.int32)
    n_slots = -(-(n_assign + n_experts * tm_e) // rows_per_step) * rows_per_step
    n_slots = -(-n_slots // tm_e) * tm_e
    flat_tok = jnp.arange(n_assign, dtype=jnp.int32) // TOP_K
    slot_tok = jnp.zeros((n_slots,), jnp.int32).at[dest].set(flat_tok, unique_indices=True)
    n_tiles = n_slots // tm_e
    tile_e = jnp.minimum(jnp.searchsorted(pend, jnp.arange(n_tiles, dtype=jnp.int32) * tm_e, side='right'),
                         n_experts - 1).astype(jnp.int32)
    n_used = (pend[-1] // tm_e).astype(jnp.int32).reshape(1)

    ys2 = _moe_ffn(tile_e, n_used, slot_tok, h2p, w1[0], b1[0], w2[0], b2[0],
                   tm=tm_e, th=min(256, d), nseg=nseg)
    tm_c = min(512, t_s)
    assert n_tok % tm_c == 0 and t_p % tm_c == 0
    dest_kmajor = dest.reshape(n_tok, TOP_K).T.reshape(-1)
    n_pad = -(-n_assign // rows_per_step) * rows_per_step
    dest_kmajor = jnp.pad(dest_kmajor, (0, n_pad - n_assign))
    yg = _gather_rows(dest_kmajor, ys2.reshape(n_slots, nseg, LANES), rows_per_step=rows_per_step)
    yg2 = yg.reshape(n_pad * nseg, LANES)

    nblk_all = n_tok // tm_c
    y_p = _combine(yg2, gates, x1_p, mods_p[5], pl.BlockSpec((None, 1, d), lambda i: (i // (seq // tm_c), 0, 0)),
                   tm=tm_c, row_off=0, nblk_all=nblk_all, nseg=nseg)
    y_s = _combine(yg2, gates, x1_s, mods_s[5], pl.BlockSpec((tm_c, d), lambda i: (i, 0)),
                   tm=tm_c, row_off=t_p // tm_c, nblk_all=nblk_all, nseg=nseg)

    return (y_p.reshape(nb_p, seq, d), y_s.reshape(nb_s, ntok, d),
            rows_p.reshape(1, nb_p, seq, cache_w), rows_s.reshape(1, nb_s, ntok, cache_w),
            st_p[None], st_s[None])
```

```python
import functools

import jax
import jax.numpy as jnp
from jax import lax
from jax.experimental import pallas as pl
from jax.experimental.pallas import tpu as pltpu

F32 = jnp.float32
BF16 = jnp.bfloat16

NORM_EPS = 1e-6
NEG_INF = -1e30
ROPE_THETA = 10000.0
TOP_K = 4
SWIGLU_LIMIT = 7.0
SWIGLU_ALPHA = 1.702
HG_CHUNK = 64
HG_SUB = 16
LANES = 128
SUBLANES = 8
MXU_WIDTH = 256
VMEM_LIMIT_BYTES = 56 * 1024 * 1024


def _cparams(*sem):
    return pltpu.CompilerParams(dimension_semantics=sem, vmem_limit_bytes=VMEM_LIMIT_BYTES)


def _pick(n, pref, mult=LANES):
    if n <= pref:
        return n
    t = pref - pref % mult
    while t > mult and n % t:
        t -= mult
    assert n % t == 0, (n, pref, mult)
    return t


def _sigmoid(x):
    return 1.0 / (1.0 + jnp.exp(-x))


def _rms(x, eps=NORM_EPS):
    return x * lax.rsqrt(jnp.mean(x * x, axis=-1, keepdims=True) + eps)


def _dot(a, b):
    return jnp.dot(a, b, preferred_element_type=F32)


def _dot_nt(a, b):
    return lax.dot_general(a, b, (((1,), (1,)), ((), ())), preferred_element_type=F32)


def _dot_tn(a, b):
    return lax.dot_general(a, b, (((0,), (0,)), ((), ())), preferred_element_type=F32)


def _adaln_kernel(c_ref, w_ref, b_ref, o_ref):
    c = c_ref[...]
    a = (c * _sigmoid(c)).astype(BF16)
    o_ref[...] = _dot(a, w_ref[...].astype(BF16)) + b_ref[...]


def _adaln(c, w, b):
    n, d = c.shape
    nout = w.shape[1]
    tn = _pick(nout, 1024)
    return pl.pallas_call(
        _adaln_kernel,
        grid=(nout // tn,),
        in_specs=[pl.BlockSpec((n, d), lambda j: (0, 0)),
                  pl.BlockSpec((d, tn), lambda j: (0, j)),
                  pl.BlockSpec((1, tn), lambda j: (0, j))],
        out_specs=pl.BlockSpec((n, tn), lambda j: (0, j)),
        out_shape=jax.ShapeDtypeStruct((n, nout), F32),
        compiler_params=_cparams("parallel"),
        name="adaln",
    )(c, w, b.reshape(1, nout))


def _inproj_kernel(x_ref, sc_ref, sh_ref, nw_ref, wa_ref, wh_ref, qaw_ref, kvw_ref, wkr_ref, wkrs_ref,
                   cos_ref, sin_ref,
                   cqn_ref, rows_ref, latb_ref, krr_ref, krsq_ref, zh_ref, h_scr, *, q_lora, kv_lora, rope):
    j = pl.program_id(1)

    @pl.when(j == 0)
    def _():
        h = _rms(x_ref[...]) * nw_ref[...]
        h = h * (1.0 + sc_ref[...]) + sh_ref[...]
        hb = h.astype(BF16)
        h_scr[...] = hb
        za = _dot(hb, wa_ref[...])
        cqn_ref[...] = (_rms(za[:, :q_lora]) * qaw_ref[...]).astype(BF16)
        lat = _rms(za[:, q_lora:q_lora + kv_lora]) * kvw_ref[...]
        o = q_lora + kv_lora
        kr_a = za[:, o:o + LANES]
        kr_b = za[:, o + LANES:o + 2 * LANES]
        rows_ref[:, :kv_lora] = lat
        rows_ref[:, kv_lora:] = kr_a[:, :rope]
        latb_ref[...] = lat.astype(BF16)
        krr_ref[...] = kr_a * wkr_ref[...] * cos_ref[...] + kr_b * wkrs_ref[...] * sin_ref[...]
        krsq_ref[...] = jnp.broadcast_to(jnp.sum(kr_a * kr_a, axis=-1, keepdims=True), krsq_ref.shape)

    @pl.when(j > 0)
    def _():
        zh_ref[...] = _dot(h_scr[...], wh_ref[...])


def _inproj(x, sc, sh, mod_spec, nw, wa, wh, qaw, kvw, wkr, wkrs, cos_t, sin_t, tab_map, *, tm,
            q_lora, kv_lora, rope):
    t, d = x.shape
    na = wa.shape[1]
    nh = wh.shape[1]
    tn = _pick(nh, 1024)
    nj = nh // tn
    const = lambda i, j: (0, 0)
    row = lambda i, j: (i, 0)
    kern = functools.partial(_inproj_kernel, q_lora=q_lora, kv_lora=kv_lora, rope=rope)
    return pl.pallas_call(
        kern,
        grid=(t // tm, 1 + nj),
        in_specs=[pl.BlockSpec((tm, d), row), mod_spec, mod_spec,
                  pl.BlockSpec((1, d), const),
                  pl.BlockSpec((d, na), const),
                  pl.BlockSpec((d, tn), lambda i, j: (0, jnp.maximum(j - 1, 0))),
                  pl.BlockSpec((1, q_lora), const), pl.BlockSpec((1, kv_lora), const),
                  pl.BlockSpec((1, LANES), const), pl.BlockSpec((1, LANES), const),
                  pl.BlockSpec((tm, LANES), tab_map), pl.BlockSpec((tm, LANES), tab_map)],
        out_specs=[pl.BlockSpec((tm, q_lora), row),
                   pl.BlockSpec((tm, kv_lora + rope), row),
                   pl.BlockSpec((tm, kv_lora), row),
                   pl.BlockSpec((tm, LANES), row),
                   pl.BlockSpec((tm, LANES), row),
                   pl.BlockSpec((tm, tn), lambda i, j: (i, jnp.maximum(j - 1, 0)))],
        out_shape=[jax.ShapeDtypeStruct((t, q_lora), BF16),
                   jax.ShapeDtypeStruct((t, kv_lora + rope), F32),
                   jax.ShapeDtypeStruct((t, kv_lora), BF16),
                   jax.ShapeDtypeStruct((t, LANES), F32),
                   jax.ShapeDtypeStruct((t, LANES), F32),
                   jax.ShapeDtypeStruct((t, nh), F32)],
        scratch_shapes=[pltpu.VMEM((tm, d), BF16)],
        compiler_params=_cparams("parallel", "arbitrary"),
        name="inproj",
    )(x, sc, sh, nw, wa, wh, qaw, kvw, wkr, wkrs, cos_t, sin_t)


def _q_kernel(cqn_ref, w_ref, wn_ref, wa_ref, wb_ref, c_ref, s_ref, o_ref, *, scale, d_head):
    y = _dot(cqn_ref[...], w_ref[...])
    nope = y[:, :LANES]
    a = y[:, LANES:2 * LANES]
    b = y[:, 2 * LANES:]
    ssq = jnp.sum(nope * nope, axis=-1, keepdims=True) + 0.5 * jnp.sum(a * a, axis=-1, keepdims=True)
    rinv = lax.rsqrt(ssq * (1.0 / d_head) + NORM_EPS) * scale
    o_ref[:, :LANES] = (nope * rinv * wn_ref[...]).astype(o_ref.dtype)
    rot = a * wa_ref[...] * c_ref[...] + b * wb_ref[...] * s_ref[...]
    o_ref[:, LANES:] = (rot * rinv).astype(o_ref.dtype)


def _queries(cqn, wq, wn, wa, wb, tabs, tab_map, *, tm, scale, d_head):
    t, r = cqn.shape
    nheads = wq.shape[0]
    wout = 2 * LANES
    const = lambda i, h: (0, 0)
    kern = functools.partial(_q_kernel, scale=scale, d_head=d_head)
    tab_spec = pl.BlockSpec((tm, LANES), lambda i, h: tab_map(i, h))
    return pl.pallas_call(
        kern,
        grid=(t // tm, nheads),
        in_specs=[pl.BlockSpec((tm, r), lambda i, h: (i, 0)),
                  pl.BlockSpec((None, r, 3 * LANES), lambda i, h: (h, 0, 0)),
                  pl.BlockSpec((1, LANES), const), pl.BlockSpec((1, LANES), const),
                  pl.BlockSpec((1, LANES), const),
                  tab_spec, tab_spec],
        out_specs=pl.BlockSpec((None, tm, wout), lambda i, h: (h, i, 0)),
        out_shape=jax.ShapeDtypeStruct((nheads, t, wout), BF16),
        compiler_params=_cparams("parallel", "parallel"),
        name="queries",
    )(cqn, wq, wn, wa, wb, *tabs)


def _kv_kernel(latb_ref, krr_ref, krsq_ref, w_ref, wn_ref, k_ref, v_ref, *, d_head):
    y = _dot(latb_ref[...], w_ref[...])
    kn = y[:, :LANES]
    ssq = jnp.sum(kn * kn, axis=-1, keepdims=True) + krsq_ref[:, :1]
    rinv = lax.rsqrt(ssq * (1.0 / d_head) + NORM_EPS)
    k_ref[:, :LANES] = (kn * rinv * wn_ref[...]).astype(BF16)
    k_ref[:, LANES:] = (krr_ref[...] * rinv).astype(BF16)
    v_ref[...] = y[:, LANES:].astype(BF16)


def _keys_values(latb, krr, krsq, wkv, wn, *, tm, d_head):
    t, c = latb.shape
    nheads = wkv.shape[0]
    row = lambda i, h: (i, 0)
    return pl.pallas_call(
        functools.partial(_kv_kernel, d_head=d_head),
        grid=(t // tm, nheads),
        in_specs=[pl.BlockSpec((tm, c), row), pl.BlockSpec((tm, LANES), row), pl.BlockSpec((tm, LANES), row),
                  pl.BlockSpec((None, c, 2 * LANES), lambda i, h: (h, 0, 0)),
                  pl.BlockSpec((1, LANES), lambda i, h: (0, 0))],
        out_specs=[pl.BlockSpec((None, tm, 2 * LANES), lambda i, h: (h, i, 0)),
                   pl.BlockSpec((None, tm, LANES), lambda i, h: (h, i, 0))],
        out_shape=[jax.ShapeDtypeStruct((nheads, t, 2 * LANES), BF16),
                   jax.ShapeDtypeStruct((nheads, t, LANES), BF16)],
        compiler_params=_cparams("parallel", "parallel"),
        name="keys_values",
    )(latb, krr, krsq, wkv, wn)


def _flash_kernel(q_ref, k_ref, v_ref, o_ref, m_scr, l_scr, acc_scr, *, tq):
    qi = pl.program_id(2)
    ki = pl.program_id(3)

    @pl.when(ki == 0)
    def _():
        m_scr[...] = jnp.full(m_scr.shape, NEG_INF, F32)
        l_scr[...] = jnp.zeros(l_scr.shape, F32)
        acc_scr[...] = jnp.zeros(acc_scr.shape, F32)

    hps = q_ref.shape[0]
    dv = v_ref.shape[-1]

    def block(on_diagonal):
        for hh in range(hps):
            s = _dot_nt(q_ref[hh], k_ref[hh])
            if on_diagonal:
                qpos = lax.broadcasted_iota(jnp.int32, s.shape, 0)
                kpos = lax.broadcasted_iota(jnp.int32, s.shape, 1)
                s = jnp.where(kpos <= qpos, s, NEG_INF)
            m_old = m_scr[hh]
            m_new = jnp.maximum(m_old, jnp.max(s, axis=-1, keepdims=True))
            alpha = jnp.exp(m_old - m_new)
            p = jnp.exp(s - m_new)
            l_scr[hh] = alpha * l_scr[hh] + jnp.sum(p, axis=-1, keepdims=True)
            acc_scr[hh] = alpha * acc_scr[hh] + _dot(p.astype(BF16), v_ref[hh])
            m_scr[hh] = m_new

    pl.when(ki < qi)(functools.partial(block, False))
    pl.when(ki == qi)(functools.partial(block, True))

    @pl.when(ki == pl.num_programs(3) - 1)
    def _():
        for hh in range(hps):
            o_ref[:, hh * dv:(hh + 1) * dv] = acc_scr[hh] / l_scr[hh]


def _flash(q, k, v, *, nbatch, seq, tq, hps):
    nheads, t, dq = q.shape
    dv = v.shape[-1]
    nq = seq // tq
    return pl.pallas_call(
        functools.partial(_flash_kernel, tq=tq),
        grid=(nbatch, nheads // hps, nq, nq),
        in_specs=[pl.BlockSpec((hps, tq, dq), lambda b, h, qi, ki: (h, b * nq + qi, 0)),
                  pl.BlockSpec((hps, tq, dq), lambda b, h, qi, ki: (h, b * nq + jnp.minimum(ki, qi), 0)),
                  pl.BlockSpec((hps, tq, dv), lambda b, h, qi, ki: (h, b * nq + jnp.minimum(ki, qi), 0))],
        out_specs=pl.BlockSpec((tq, hps * dv), lambda b, h, qi, ki: (b * nq + qi, h)),
        out_shape=jax.ShapeDtypeStruct((t, nheads * dv), F32),
        scratch_shapes=[pltpu.VMEM((hps, tq, 1), F32), pltpu.VMEM((hps, tq, 1), F32),
                        pltpu.VMEM((hps, tq, dv), F32)],
        compiler_params=_cparams("parallel", "parallel", "parallel", "arbitrary"),
        name="prompt_attention",
    )(q, k, v)


def _absorb_kernel(q_ref, wuk_ref, wkn_ref, wkr_ref, o_ref, *, kv_lora):
    q = q_ref[...].astype(F32)
    qn = (q[:, :LANES] * wkn_ref[...]).astype(BF16)
    o_ref[:, :kv_lora] = _dot_nt(qn, wuk_ref[...]).astype(BF16)
    o_ref[:, kv_lora:] = (q[:, LANES:] * wkr_ref[...]).astype(BF16)


def _absorb(q, wuk, wkn, wkr, *, kv_lora):
    nheads, t, wq = q.shape
    wout = kv_lora + LANES
    const = lambda h: (0, 0)
    return pl.pallas_call(
        functools.partial(_absorb_kernel, kv_lora=kv_lora),
        grid=(nheads,),
        in_specs=[pl.BlockSpec((None, t, wq), lambda h: (h, 0, 0)),
                  pl.BlockSpec((None, kv_lora, LANES), lambda h: (h, 0, 0)),
                  pl.BlockSpec((1, LANES), const), pl.BlockSpec((1, LANES), const)],
        out_specs=pl.BlockSpec((t, wout), lambda h: (0, h)),
        out_shape=jax.ShapeDtypeStruct((t, nheads * wout), BF16),
        compiler_params=_cparams("parallel"),
        name="absorb_queries",
    )(q, wuk, wkn, wkr)


def _sattn_kernel(pt_ref, *refs, n_group, kv_lora, rope, nheads, ntok, d_head, page):
    page_refs = refs[:n_group]
    (new_ref, qf_ref, wukt_ref, cs_ref, csn_ref, o_ref,
     lhs_scr, latb_scr, m_scr, l_scr, acc_scr) = refs[n_group:]
    step = pl.program_id(1)
    nrow = nheads * ntok
    nk_w = wukt_ref.shape[0]

    @pl.when(step == 0)
    def _():
        lhs_scr[:nk_w, :] = wukt_ref[...]
        lhs_scr[nk_w:, :] = qf_ref[:, :kv_lora]
        m_scr[...] = jnp.full(m_scr.shape, NEG_INF, F32)
        l_scr[...] = jnp.zeros(l_scr.shape, F32)
        acc_scr[...] = jnp.zeros(acc_scr.shape, F32)

    eye = (lax.broadcasted_iota(jnp.int32, (nrow, nrow), 0)
           == lax.broadcasted_iota(jnp.int32, (nrow, nrow), 1))

    def as_row(col):
        return jnp.sum(jnp.where(eye, col, 0.0), axis=0, keepdims=True)

    def scores(lat_b, kr_t, cs):
        nk = lat_b.shape[1]
        r = _dot(lhs_scr[...], lat_b)
        kn = r[:nk_w]
        kn2 = jnp.sum((kn * kn).reshape(nheads, nk_w // nheads, nk), axis=1)
        krsq = jnp.sum(kr_t * kr_t, axis=0, keepdims=True)
        rinv = lax.rsqrt((kn2 + krsq) * (1.0 / d_head) + NORM_EPS)
        rinv_rows = jnp.concatenate([rinv] * ntok, axis=0)
        f2 = (jnp.concatenate([kr_t, kr_t], axis=0) * cs).astype(BF16)
        return (r[nk_w:] + _dot(qf_ref[:, kv_lora:], f2)) * rinv_rows

    def update(s, lat_b):
        m_old = m_scr[...]
        m_new = jnp.maximum(m_old, jnp.max(s, axis=-1, keepdims=True))
        alpha = jnp.exp(m_old - m_new)
        p = jnp.exp(s - m_new)
        l_scr[...] = alpha * l_scr[...] + jnp.sum(p, axis=-1, keepdims=True)
        acc_scr[...] = acc_scr[...] * as_row(alpha) + _dot_nt(lat_b, p.astype(BF16))
        m_scr[...] = m_new

    per = max(1, min(n_group, MXU_WIDTH // page))
    s_parts = []
    for g in range(0, n_group, per):
        cols = slice(g * page, (g + per) * page)
        lat_b = jnp.concatenate([page_refs[g + u][:kv_lora, :] for u in range(per)], axis=1).astype(BF16)
        kr_t = jnp.concatenate([page_refs[g + u][kv_lora:, :] for u in range(per)], axis=1)
        latb_scr[:, cols] = lat_b
        s_parts.append(scores(lat_b, kr_t, cs_ref[:, cols]))
    update(jnp.concatenate(s_parts, axis=1), latb_scr[...])

    @pl.when(step == pl.num_programs(1) - 1)
    def _():
        key = lax.broadcasted_iota(jnp.int32, (nrow, page), 1)
        tok = lax.broadcasted_iota(jnp.int32, (nrow, page), 0) // nheads
        lat_b = new_ref[:kv_lora, :].astype(BF16)
        s_new = scores(lat_b, new_ref[kv_lora:, :], csn_ref[...])
        update(jnp.where(key <= tok, s_new, NEG_INF), lat_b)
        o_ref[...] = acc_scr[...] / as_row(l_scr[...])


def _sample_attention(page_table, cache_t, new_pages_t, qf, wukt, cs_pages, cs_new, *,
                      n_group, kv_lora, rope, nheads, ntok, d_head):
    nb, n_pages = page_table.shape
    cw, page = cache_t.shape[2], cache_t.shape[3]
    nrow = nheads * ntok
    wqf = qf.shape[-1]
    nsteps = n_pages // n_group

    def page_map(g):
        return lambda b, s, pt: (0, pt[b * n_pages + s * n_group + g], 0, 0)

    kern = functools.partial(_sattn_kernel, n_group=n_group, kv_lora=kv_lora, rope=rope, nheads=nheads,
                             ntok=ntok, d_head=d_head, page=page)
    in_specs = [pl.BlockSpec((None, None, cw, page), page_map(g)) for g in range(n_group)]
    in_specs += [pl.BlockSpec((None, cw, page), lambda b, s, pt: (b, 0, 0)),
                 pl.BlockSpec((None, nrow, wqf), lambda b, s, pt: (b, 0, 0)),
                 pl.BlockSpec(wukt.shape, lambda b, s, pt: (0, 0)),
                 pl.BlockSpec((2 * rope, n_group * page), lambda b, s, pt: (0, s)),
                 pl.BlockSpec((2 * rope, page), lambda b, s, pt: (0, 0))]
    grid_spec = pltpu.PrefetchScalarGridSpec(
        num_scalar_prefetch=1,
        grid=(nb, nsteps),
        in_specs=in_specs,
        out_specs=pl.BlockSpec((None, kv_lora, nrow), lambda b, s, pt: (b, 0, 0)),
        scratch_shapes=[pltpu.VMEM((wukt.shape[0] + nrow, kv_lora), BF16),
                        pltpu.VMEM((kv_lora, n_group * page), BF16),
                        pltpu.VMEM((nrow, 1), F32), pltpu.VMEM((nrow, 1), F32),
                        pltpu.VMEM((kv_lora, nrow), F32)])
    return pl.pallas_call(
        kern,
        grid_spec=grid_spec,
        out_shape=jax.ShapeDtypeStruct((nb, kv_lora, nrow), F32),
        compiler_params=_cparams("parallel", "arbitrary"),
        name="sample_attention",
    )(page_table.reshape(-1), *([cache_t] * n_group), new_pages_t, qf, wukt, cs_pages, cs_new)


def _uv_kernel(lm_ref, w_ref, o_ref):
    o_ref[...] = _dot(lm_ref[...].astype(BF16), w_ref[...])


def _uv_project(lm, wuv):
    t = lm.shape[0]
    nheads, c, dv = wuv.shape
    return pl.pallas_call(
        _uv_kernel,
        grid=(nheads,),
        in_specs=[pl.BlockSpec((t, c), lambda h: (0, h)),
                  pl.BlockSpec((None, c, dv), lambda h: (h, 0, 0))],
        out_specs=pl.BlockSpec((t, dv), lambda h: (0, h)),
        out_shape=jax.ShapeDtypeStruct((t, nheads * dv), F32),
        compiler_params=_cparams("parallel"),
        name="value_up_projection",
    )(lm, wuv)


def _cumsum_rows(g, chunk):
    if chunk <= SUBLANES:
        ridx = lax.broadcasted_iota(jnp.int32, g.shape, 0)
        out = jnp.zeros_like(g)
        for s in range(chunk):
            out = out + jnp.where(ridx >= s, g[s:s + 1, :], 0.0)
        return out
    rows = lax.broadcasted_iota(jnp.int32, (chunk, chunk), 0)
    cols = lax.broadcasted_iota(jnp.int32, (chunk, chunk), 1)
    tri = (cols <= rows).astype(BF16)
    hi = g.astype(BF16)
    r1 = g - hi.astype(F32)
    mid = r1.astype(BF16)
    lo = (r1 - mid.astype(F32)).astype(BF16)
    return _dot(tri, hi) + (_dot(tri, mid) + _dot(tri, lo))


def _hgrn_kernel(hq_ref, hf_ref, hi_ref, hg_ref, lbp_ref, s0_ref, gnw_ref, o_ref, sout_ref, st_scr, *,
                 chunk, sub, nchunk, t_valid, has_s0, hps):
    tb = pl.program_id(2)
    last = tb == pl.num_programs(2) - 1
    gnw = gnw_ref[...]

    @pl.when(tb == 0)
    def _():
        for hh in range(hps):
            if has_s0:
                st_scr[hh] = s0_ref[hh].T
            else:
                st_scr[hh] = jnp.zeros(st_scr.shape[1:], F32)

    finals = []
    for hh in range(hps):
        cs = slice(hh * LANES, (hh + 1) * LANES)

        a = lbp_ref[:, cs]
        e = jnp.exp(a - jnp.max(a, axis=0, keepdims=True))
        lb = e[0:1, :] / jnp.sum(e, axis=0, keepdims=True)

        st = st_scr[hh]
        for c in range(nchunk):
            sl = slice(c * chunk, (c + 1) * chunk)
            zq = hq_ref[sl, cs]
            q = zq * _sigmoid(zq)
            k = (1.0 - lb) * _sigmoid(-hf_ref[sl, cs])
            g = jnp.log(1.0 - k)
            if t_valid is not None:
                valid = (tb * (nchunk * chunk) + c * chunk
                         + lax.broadcasted_iota(jnp.int32, k.shape, 0)) < t_valid
                k = jnp.where(valid, k, 0.0)
                g = jnp.where(valid, g, 0.0)
            v = hi_ref[sl, cs]
            zg = hg_ref[sl, cs]
            gate = zg * _sigmoid(zg)

            gc = _cumsum_rows(g, chunk)
            o = _dot_nt(q * jnp.exp(gc), st)
            g_last = gc[chunk - 1:chunk, :]
            u_t = _dot_tn(v, k * jnp.exp(g_last - gc))

            parts = []
            for i in range(chunk // sub):
                r0 = i * sub
                gi = gc[r0:r0 + sub, :]
                qi = q[r0:r0 + sub, :]
                if i > 0:
                    ref_row = gc[r0 - 1:r0, :]
                    a_off = _dot_nt(qi * jnp.exp(gi - ref_row), k[:r0, :] * jnp.exp(ref_row - gc[:r0, :]))
                    o_i = _dot(a_off, v[:r0, :])
                else:
                    o_i = jnp.zeros((sub, v.shape[1]), F32)
                trow = lax.broadcasted_iota(jnp.int32, (sub, 1), 0)
                for s in range(sub):
                    keep = trow >= s
                    d = jnp.where(keep, gi - gc[r0 + s:r0 + s + 1, :], 0.0)
                    w = jnp.exp(d) * qi * k[r0 + s:r0 + s + 1, :]
                    a_col = jnp.where(keep, jnp.sum(w, axis=-1, keepdims=True), 0.0)
                    o_i = o_i + a_col * v[r0 + s:r0 + s + 1, :]
                parts.append(o_i)
            o = o + (parts[0] if len(parts) == 1 else jnp.concatenate(parts, axis=0))
            st = st * jnp.exp(g_last) + u_t
            o_ref[sl, cs] = _rms(o) * gnw * gate

        st_scr[hh] = st
        finals.append(st)

    @pl.when(last)
    def _():
        for hh in range(hps):
            sout_ref[hh] = finals[hh].T


def _hgrn(zh, lbp, s0, gnw, *, nheads, hps, tblock, chunk, sub, t_valid):
    n, t, _ = zh.shape
    dk = LANES
    has_s0 = s0 is not None
    nlb = lbp.shape[0]
    ngrp = nheads // hps
    if not has_s0:
        s0 = jnp.zeros((1, hps, dk, dk), F32)
        s0_spec = pl.BlockSpec((None, hps, dk, dk), lambda b, h, tb: (0, 0, 0, 0))
    else:
        s0_spec = pl.BlockSpec((None, hps, dk, dk), lambda b, h, tb: (b, h, 0, 0))

    def col(group):
        return pl.BlockSpec((None, tblock, hps * dk), lambda b, h, tb: (b, tb, group * ngrp + h))

    kern = functools.partial(_hgrn_kernel, chunk=chunk, sub=sub, nchunk=tblock // chunk, t_valid=t_valid,
                             has_s0=has_s0, hps=hps)
    return pl.pallas_call(
        kern,
        grid=(n, ngrp, t // tblock),
        in_specs=[col(0), col(1), col(2), col(3),
                  pl.BlockSpec((nlb, hps * dk), lambda b, h, tb: (0, h)),
                  s0_spec,
                  pl.BlockSpec((1, dk), lambda b, h, tb: (0, 0))],
        out_specs=[pl.BlockSpec((None, tblock, hps * dk), lambda b, h, tb: (b, tb, h)),
                   pl.BlockSpec((None, hps, dk, dk), lambda b, h, tb: (b, h, 0, 0))],
        out_shape=[jax.ShapeDtypeStruct((n, t, nheads * dk), F32),
                   jax.ShapeDtypeStruct((n, nheads, dk, dk), F32)],
        scratch_shapes=[pltpu.VMEM((hps, dk, dk), F32)],
        compiler_params=_cparams("parallel", "parallel", "arbitrary"),
        name="hgrn2",
    )(zh, zh, zh, zh, lbp, s0, gnw)


def _outproj_kernel(oa_ref, orec_ref, x_ref, ga_ref, scf_ref, shf_ref, gnw_ref, fnw_ref, wo_ref, rw_ref,
                    rb_ref, x1_ref, h2_ref, topi_ref, gate_ref, *, n_attn):
    oan = (_rms(oa_ref[...]) * gnw_ref[...]).astype(BF16)
    mix = _dot(oan, wo_ref[:n_attn, :]) + _dot(orec_ref[...].astype(BF16), wo_ref[n_attn:, :])
    x1 = x_ref[...] + ga_ref[...] * mix
    x1_ref[...] = x1
    h2 = _rms(x1) * fnw_ref[...] * (1.0 + scf_ref[...]) + shf_ref[...]
    h2_ref[...] = h2
    logits = jnp.dot(h2, rw_ref[...], preferred_element_type=F32,
                     precision=lax.Precision.HIGHEST) + rb_ref[...]
    lane = lax.broadcasted_iota(jnp.int32, logits.shape, 1).astype(F32)
    vals = []
    topi = jnp.zeros(logits.shape, F32)
    work = logits
    for kk in range(TOP_K):
        m = jnp.max(work, axis=-1, keepdims=True)
        idx = jnp.min(jnp.where(work == m, lane, float(LANES)), axis=-1, keepdims=True)
        vals.append(m)
        topi = jnp.where(lane == float(kk), idx, topi)
        work = jnp.where(lane == idx, -3e38, work)
    es = [jnp.exp(vv - vals[0]) for vv in vals]
    den = es[0]
    for ee in es[1:]:
        den = den + ee
    gates = jnp.zeros(logits.shape, F32)
    for kk in range(TOP_K):
        gates = jnp.where(lane == float(kk), es[kk] / den, gates)
    topi_ref[...] = topi.astype(jnp.int32)
    gate_ref[...] = gates


def _outproj(oa, orec, x, ga, scf, shf, mod_spec, gnw, fnw, wo, rw, rb, *, tm):
    t, d = x.shape
    n_attn = oa.shape[1]
    n_rec = orec.shape[1]
    row = lambda i: (i, 0)
    const = lambda i: (0, 0)
    return pl.pallas_call(
        functools.partial(_outproj_kernel, n_attn=n_attn),
        grid=(t // tm,),
        in_specs=[pl.BlockSpec((tm, n_attn), row), pl.BlockSpec((tm, n_rec), row), pl.BlockSpec((tm, d), row),
                  mod_spec, mod_spec, mod_spec,
                  pl.BlockSpec((1, n_attn), const), pl.BlockSpec((1, d), const),
                  pl.BlockSpec(wo.shape, const), pl.BlockSpec(rw.shape, const), pl.BlockSpec((1, LANES), const)],
        out_specs=[pl.BlockSpec((tm, d), row), pl.BlockSpec((tm, d), row),
                   pl.BlockSpec((tm, LANES), row), pl.BlockSpec((tm, LANES), row)],
        out_shape=[jax.ShapeDtypeStruct((t, d), F32), jax.ShapeDtypeStruct((t, d), F32),
                   jax.ShapeDtypeStruct((t, LANES), jnp.int32), jax.ShapeDtypeStruct((t, LANES), F32)],
        compiler_params=_cparams("parallel"),
        name="outproj_router",
    )(oa, orec, x, ga, scf, shf, gnw, fnw, wo, rw, rb)


def _gather_kernel(idx_ref, src_ref, dst_ref, sem, *, rows_per_step):
    def row_copy(r, src_row):
        return pltpu.make_async_copy(src_ref.at[src_row], dst_ref.at[r], sem)

    def issue(g, carry):
        for u in range(SUBLANES):
            r = g * SUBLANES + u
            row_copy(r, idx_ref[r]).start(priority=u % 2)
        return carry

    def drain(r, carry):
        row_copy(r, 0).wait()
        return carry

    lax.fori_loop(0, rows_per_step // SUBLANES, issue, 0)
    lax.fori_loop(0, rows_per_step, drain, 0)


def _gather_rows(idx, src, *, rows_per_step):
    n = idx.shape[0]
    return pl.pallas_call(
        functools.partial(_gather_kernel, rows_per_step=rows_per_step),
        grid=(n // rows_per_step,),
        in_specs=[pl.BlockSpec((rows_per_step,), lambda i: (i,), memory_space=pltpu.SMEM),
                  pl.BlockSpec(memory_space=pl.ANY)],
        out_specs=pl.BlockSpec((rows_per_step,) + src.shape[1:], lambda i: (i, 0, 0)),
        out_shape=jax.ShapeDtypeStruct((n,) + src.shape[1:], src.dtype),
        scratch_shapes=[pltpu.SemaphoreType.DMA(())],
        compiler_params=_cparams("parallel"),
        name="gather_rows",
    )(idx, src)


def _moe_kernel(te_ref, nu_ref, idx0_ref, idxn_ref, h_ref, w1g_ref, w1l_ref, b1g_ref, b1l_ref, w2_ref, b2_ref,
                o_ref, xbuf, xb_scr, acc_scr, sem, *, nseg, nj):
    i = pl.program_id(0)
    j = pl.program_id(1)
    tm = xb_scr.shape[0]
    npk = nseg // 2
    half = npk * LANES
    per = tm // nj
    nu = nu_ref[0]

    def row_copy(slot, r, src_row):
        dst = xbuf.at[pl.ds(pl.multiple_of((slot * tm + r) * npk, npk), npk), :]
        return pltpu.make_async_copy(h_ref.at[src_row], dst, sem.at[slot])

    def issue_rows(slot, idx_ref, start, count):
        def body(r, carry):
            row_copy(slot, r, idx_ref[r]).start()
            return carry
        lax.fori_loop(start, start + count, body, 0)

    def wait_rows(slot):
        def body(r, carry):
            row_copy(slot, r, 0).wait()
            return carry
        lax.fori_loop(0, tm, body, 0)

    @pl.when(jnp.logical_and(i == 0, j == 0))
    def _():
        issue_rows(0, idx0_ref, 0, tm)

    for slot in (0, 1):
        @pl.when(jnp.logical_and(i % 2 == slot, i < nu))
        def _():
            @pl.when(j == 0)
            def _():
                wait_rows(slot)
                for s in range(npk):
                    u = xbuf[pl.ds(slot * tm * npk + s, tm, stride=npk), :]
                    lo = lax.bitcast_convert_type(u << 16, F32)
                    hi = lax.bitcast_convert_type(u & jnp.uint32(0xFFFF0000), F32)
                    xb_scr[:, s * LANES:(s + 1) * LANES] = lo.astype(BF16)
                    xb_scr[:, half + s * LANES:half + (s + 1) * LANES] = hi.astype(BF16)
                acc_scr[...] = jnp.broadcast_to(b2_ref[...], acc_scr.shape)

    @pl.when(i < nu)
    def _():
        nxt = (i + 1) % 2
        for u in range(per):
            r = j * per + u
            row_copy(nxt, r, idxn_ref[r]).start(priority=u % 2)

        xb = xb_scr[...]
        yg = _dot(xb, w1g_ref[...].astype(BF16)) + b1g_ref[...]
        yl = _dot(xb, w1l_ref[...].astype(BF16)) + b1l_ref[...]
        glu = jnp.minimum(yg, SWIGLU_LIMIT)
        lin = jnp.clip(yl, -SWIGLU_LIMIT, SWIGLU_LIMIT)
        a = (glu * _sigmoid(SWIGLU_ALPHA * glu) * (lin + 1.0)).astype(BF16)
        acc_scr[...] += _dot(a, w2_ref[...].astype(BF16))

        @pl.when(j == nj - 1)
        def _():
            for s in range(nseg):
                o_ref[pl.ds(s, tm, stride=nseg), :] = acc_scr[:, s * LANES:(s + 1) * LANES]

        @pl.when(jnp.logical_and(i == nu - 1, j == nj - 1))
        def _():
            wait_rows(nxt)

    @pl.when(jnp.logical_and(i >= nu, j == 0))
    def _():
        o_ref[...] = jnp.zeros(o_ref.shape, F32)


def _moe_ffn(tile_e, n_used, slot_tok, h2p, w1, b1, w2, b2, *, tm, th, nseg):
    d = nseg * LANES
    p = slot_tok.shape[0]
    ne, _, two_de = w1.shape
    de = two_de // 2
    nj = de // th
    ntiles = p // tm
    assert tm % nj == 0 and nseg % 2 == 0

    def tile(i, nu):
        return jnp.minimum(i, nu[0] - 1)

    def hid(i, j, nu):
        return jnp.where(i < nu[0], j, nj - 1)

    grid_spec = pltpu.PrefetchScalarGridSpec(
        num_scalar_prefetch=2,
        grid=(ntiles, nj),
        in_specs=[pl.BlockSpec((tm,), lambda i, j, te, nu: (0,), memory_space=pltpu.SMEM),
                  pl.BlockSpec((tm,), lambda i, j, te, nu: (jnp.minimum(i + 1, ntiles - 1),),
                               memory_space=pltpu.SMEM),
                  pl.BlockSpec(memory_space=pl.ANY),
                  pl.BlockSpec((None, d, th), lambda i, j, te, nu: (te[tile(i, nu)], 0, hid(i, j, nu))),
                  pl.BlockSpec((None, d, th), lambda i, j, te, nu: (te[tile(i, nu)], 0, nj + hid(i, j, nu))),
                  pl.BlockSpec((None, 1, th), lambda i, j, te, nu: (te[tile(i, nu)], 0, hid(i, j, nu))),
                  pl.BlockSpec((None, 1, th), lambda i, j, te, nu: (te[tile(i, nu)], 0, nj + hid(i, j, nu))),
                  pl.BlockSpec((None, th, d), lambda i, j, te, nu: (te[tile(i, nu)], hid(i, j, nu), 0)),
                  pl.BlockSpec((None, 1, d), lambda i, j, te, nu: (te[tile(i, nu)], 0, 0))],
        out_specs=pl.BlockSpec((tm * nseg, LANES), lambda i, j, te, nu: (i, 0)),
        scratch_shapes=[pltpu.VMEM((2 * tm * (nseg // 2), LANES), jnp.uint32),
                        pltpu.VMEM((tm, d), BF16),
                        pltpu.VMEM((tm, d), F32),
                        pltpu.SemaphoreType.DMA((2,))])
    return pl.pallas_call(
        functools.partial(_moe_kernel, nseg=nseg, nj=nj),
        grid_spec=grid_spec,
        out_shape=jax.ShapeDtypeStruct((p * nseg, LANES), F32),
        compiler_params=_cparams("arbitrary", "arbitrary"),
        name="moe_experts",
    )(tile_e, n_used, slot_tok, slot_tok, h2p, w1, w1, b1.reshape(ne, 1, two_de),
      b1.reshape(ne, 1, two_de), w2, b2.reshape(ne, 1, d))


def _combine_kernel(*refs, nseg):
    y_refs = refs[:TOP_K]
    gate_ref, x1_ref, gf_ref, o_ref = refs[TOP_K:]
    tm = x1_ref.shape[0]
    gates = gate_ref[...]
    gk = [gates[:, kk:kk + 1] for kk in range(TOP_K)]
    for s in range(nseg):
        seg = slice(s * LANES, (s + 1) * LANES)
        acc = gk[0] * y_refs[0][pl.ds(s, tm, stride=nseg), :]
        for kk in range(1, TOP_K):
            acc = acc + gk[kk] * y_refs[kk][pl.ds(s, tm, stride=nseg), :]
        o_ref[:, seg] = x1_ref[:, seg] + gf_ref[:, seg] * acc


def _combine(yg2, gates, x1, gf, mod_spec, *, tm, row_off, nblk_all, nseg):
    t, d = x1.shape

    def ymap(kk):
        return lambda i: (kk * nblk_all + row_off + i, 0)

    return pl.pallas_call(
        functools.partial(_combine_kernel, nseg=nseg),
        grid=(t // tm,),
        in_specs=[pl.BlockSpec((tm * nseg, LANES), ymap(kk)) for kk in range(TOP_K)] + [
            pl.BlockSpec((tm, LANES), lambda i: (i + row_off, 0)),
            pl.BlockSpec((tm, d), lambda i: (i, 0)),
            mod_spec],
        out_specs=pl.BlockSpec((tm, d), lambda i: (i, 0)),
        out_shape=jax.ShapeDtypeStruct((t, d), F32),
        compiler_params=_cparams("parallel"),
        name="moe_combine",
    )(*([yg2] * TOP_K), gates, x1, gf)


def _rope_angles(pos, half):
    inv = ROPE_THETA ** (-jnp.arange(half, dtype=F32) / half)
    ang = pos.astype(F32)[:, None] * inv[None, :]
    return jnp.cos(ang), jnp.sin(ang)


def _pad_lanes(v, width=LANES):
    v = v.reshape(1, -1).astype(F32)
    return jnp.pad(v, ((0, 0), (0, width - v.shape[1])))


def _swap_halves(v):
    h = v.shape[-1] // 2
    return jnp.concatenate([v[..., h:], v[..., :h]], -1)


def kernel(x_prompt, x_sample, cache_mla, state_hgrn, page_table, c_prompt, c_sample, w_ada, b_ada,
           attn_norm_w, w_in, q_a_norm_w, w_uq, kv_a_norm_w, w_ukv, q_head_norm_w, k_head_norm_w,
           attn_group_norm_w, hg_lower_bound, hg_norm_w, w_out, ffn_norm_w, router_w, router_b,
           w1, b1, w2, b2):
    nb_p, seq, d = x_prompt.shape
    nb_s, ntok, _ = x_sample.shape
    depth = w_ada.shape[0]
    assert depth == 1, "single-layer step"
    page = cache_mla.shape[2]
    cache_w = cache_mla.shape[3]
    n_pages = page_table.shape[1]
    past_len = n_pages * page
    q_lora, nheads, d_head = w_uq.shape[1], w_uq.shape[2], w_uq.shape[3]
    kv_lora = w_ukv.shape[1]
    rope = cache_w - kv_lora
    nope = d_head - rope
    v_head = w_ukv.shape[3] - nope
    hg_heads, hg_dk, hg_dv = state_hgrn.shape[2], state_hgrn.shape[3], state_hgrn.shape[4]
    n_experts = router_w.shape[2]
    assert nope == LANES and v_head == LANES and hg_dk == LANES and hg_dv == LANES
    assert 2 * rope == LANES and n_experts <= LANES and d % LANES == 0
    half = rope // 2
    nseg = d // LANES
    attn_scale = d_head ** -0.5
    t_p = nb_p * seq
    t_s = nb_s * ntok

    mod = _adaln(jnp.concatenate([c_prompt, c_sample], 0), w_ada[0], b_ada[0])
    mods = [mod[:, i * d:(i + 1) * d] for i in range(6)]
    mods_p = [m[:nb_p].reshape(nb_p, 1, d) for m in mods]
    mods_s = [jnp.repeat(m[nb_p:], ntok, axis=0) for m in mods]

    wi = w_in[0]
    o_kr = q_lora + kv_lora
    w_kr = wi[:, o_kr:o_kr + rope]
    zpad = jnp.zeros((d, LANES - rope), F32)
    wa = jnp.concatenate([wi[:, :o_kr], w_kr, zpad, _swap_halves(w_kr), zpad], 1).astype(BF16)
    wh = wi[:, o_kr + rope:].astype(BF16)
    wq_full = jnp.transpose(w_uq[0], (1, 0, 2))
    wq_r = wq_full[:, :, nope:]
    wq_rs = _swap_halves(wq_r)
    wq = jnp.concatenate([wq_full[:, :, :nope], wq_r, wq_rs, wq_rs, wq_r], -1).astype(BF16)
    wkv = jnp.transpose(w_ukv[0], (1, 0, 2)).astype(BF16)
    wuk = wkv[:, :, :nope]
    wuv = wkv[:, :, nope:]
    wukt = jnp.transpose(wuk, (0, 2, 1)).reshape(nheads * nope, kv_lora)
    qn_w, kn_w = q_head_norm_w[0], k_head_norm_w[0]
    qr_w, kr_w = qn_w[nope:], kn_w[nope:]
    qw_n = _pad_lanes(qn_w[:nope])
    qw_a = jnp.concatenate([qr_w, _swap_halves(qr_w)]).reshape(1, LANES)
    qw_b = jnp.concatenate([_swap_halves(qr_w), qr_w]).reshape(1, LANES)
    kw_n, kw_a, kw_b = _pad_lanes(kn_w[:nope]), _pad_lanes(kr_w), _pad_lanes(_swap_halves(kr_w))
    kw_rr = jnp.concatenate([kr_w, kr_w]).reshape(1, LANES)
    nw = attn_norm_w[0].reshape(1, d)
    qaw = q_a_norm_w[0].reshape(1, q_lora)
    kvw = kv_a_norm_w[0].reshape(1, kv_lora)

    def token_tables(pos):
        cos, sin = _rope_angles(pos, half)
        z = jnp.zeros((pos.shape[0], LANES - rope), F32)
        k_cos = jnp.concatenate([cos, cos, z], -1)
        k_sin = jnp.concatenate([-sin, sin, z], -1)
        q_cos = jnp.concatenate([cos, cos, cos, -cos], -1)
        q_sin = jnp.concatenate([-sin, sin, sin, sin], -1)
        return k_cos, k_sin, q_cos, q_sin

    tabs_p = token_tables(jnp.arange(seq))
    tabs_s = token_tables(past_len + (jnp.arange(t_s) % ntok))

    tm_p = min(512, seq)
    tm_s = t_s
    nblk = seq // tm_p
    modspec_p2 = pl.BlockSpec((None, 1, d), lambda i, j: (i // nblk, 0, 0))
    modspec_s2 = pl.BlockSpec((tm_s, d), lambda i, j: (i, 0))
    dims = dict(q_lora=q_lora, kv_lora=kv_lora, rope=rope)

    xp = x_prompt.reshape(t_p, d)
    xs = x_sample.reshape(t_s, d)
    cqn_p, rows_p, latb_p, krr_p, krsq_p, zh_p = _inproj(
        xp, mods_p[1], mods_p[0], modspec_p2, nw, wa, wh, qaw, kvw, kw_a, kw_b, tabs_p[0], tabs_p[1],
        lambda i, j: (i % nblk, 0), tm=tm_p, **dims)
    cqn_s, rows_s, latb_s, krr_s, krsq_s, zh_s = _inproj(
        xs, mods_s[1], mods_s[0], modspec_s2, nw, wa, wh, qaw, kvw, kw_a, kw_b, tabs_s[0], tabs_s[1],
        lambda i, j: (i, 0), tm=tm_s, **dims)

    q_p = _queries(cqn_p, wq, qw_n, qw_a, qw_b, tabs_p[2:], lambda i, h: (i % nblk, 0), tm=tm_p,
                   scale=attn_scale, d_head=d_head)
    k_p, v_p = _keys_values(latb_p, krr_p, krsq_p, wkv, kw_n, tm=tm_p, d_head=d_head)
    oa_p = _flash(q_p, k_p, v_p, nbatch=nb_p, seq=seq, tq=tm_p, hps=2 if nheads % 2 == 0 else 1)

    q_s = _queries(cqn_s, wq, qw_n, qw_a, qw_b, tabs_s[2:], lambda i, h: (i, 0), tm=tm_s,
                   scale=attn_scale, d_head=d_head)
    qf = _absorb(q_s, wuk, kw_n, kw_rr, kv_lora=kv_lora)
    qf = qf.reshape(nb_s, ntok * nheads, kv_lora + LANES)

    def cs_table(pos):
        cos, sin = _rope_angles(pos, half)
        return jnp.concatenate([cos, cos, sin, sin], -1).T

    cs_pages = cs_table(jnp.arange(past_len))
    cs_new = cs_table(past_len + jnp.arange(page))
    cache_t = jnp.swapaxes(cache_mla, 2, 3)
    new_pages = jnp.zeros((nb_s, page, cache_w), F32).at[:, :ntok].set(rows_s.reshape(nb_s, ntok, cache_w))
    new_pages_t = jnp.swapaxes(new_pages, 1, 2)
    n_group = 8
    while n_pages % n_group:
        n_group //= 2
    lat_mix = _sample_attention(page_table, cache_t, new_pages_t, qf, wukt, cs_pages, cs_new,
                                n_group=n_group, kv_lora=kv_lora, rope=rope, nheads=nheads, ntok=ntok,
                                d_head=d_head)
    lat_mix = jnp.swapaxes(lat_mix, 1, 2)
    oa_s = _uv_project(lat_mix.reshape(t_s, nheads * kv_lora), wuv)

    gnw_h = hg_norm_w[0].reshape(1, hg_dv)
    chunk_p = min(HG_CHUNK, seq)
    orec_p, st_p = _hgrn(zh_p.reshape(nb_p, seq, -1), hg_lower_bound, None, gnw_h, nheads=hg_heads,
                         hps=2 if hg_heads % 2 == 0 else 1,
                         tblock=min(256, seq), chunk=chunk_p, sub=min(HG_SUB, chunk_p), t_valid=None)
    t_pad = -(-ntok // SUBLANES) * SUBLANES
    zh_s3 = jnp.pad(zh_s.reshape(nb_s, ntok, -1), ((0, 0), (0, t_pad - ntok), (0, 0)))
    orec_s, st_s = _hgrn(zh_s3, hg_lower_bound, state_hgrn[0], gnw_h, nheads=hg_heads, hps=hg_heads,
                         tblock=t_pad, chunk=t_pad, sub=t_pad, t_valid=ntok)
    orec_s = orec_s[:, :ntok].reshape(t_s, -1)

    wo = w_out[0].astype(BF16)
    rw = jnp.pad(router_w[0], ((0, 0), (0, LANES - n_experts)))
    rb = jnp.concatenate([router_b[0], jnp.full((LANES - n_experts,), NEG_INF, F32)]).reshape(1, LANES)
    gnw_a = attn_group_norm_w[0].reshape(1, -1)
    fnw = ffn_norm_w[0].reshape(1, d)
    modspec_p1 = pl.BlockSpec((None, 1, d), lambda i: (i // nblk, 0, 0))
    modspec_s1 = pl.BlockSpec((tm_s, d), lambda i: (i, 0))
    x1_p, h2_p, ti_p, gt_p = _outproj(oa_p, orec_p.reshape(t_p, -1), xp, mods_p[2], mods_p[4], mods_p[3],
                                      modspec_p1, gnw_a, fnw, wo, rw, rb, tm=tm_p)
    x1_s, h2_s, ti_s, gt_s = _outproj(oa_s, orec_s, xs, mods_s[2], mods_s[4], mods_s[3],
                                      modspec_s1, gnw_a, fnw, wo, rw, rb, tm=tm_s)

    n_tok = t_p + t_s
    h2b = jnp.concatenate([h2_p, h2_s], 0).astype(BF16)
    h2w = lax.bitcast_convert_type(h2b, jnp.uint16).astype(jnp.uint32)
    h2p = (h2w[:, :d // 2] | (h2w[:, d // 2:] << 16)).reshape(n_tok, nseg // 2, LANES)
    top_i = jnp.concatenate([ti_p[:, :TOP_K], ti_s[:, :TOP_K]], 0)
    gates = jnp.concatenate([gt_p, gt_s], 0)
    rows_per_step = 1024
    tm_e = rows_per_step if n_tok >= rows_per_step else 1 << (n_tok.bit_length() - 1)
    n_assign = n_tok * TOP_K
    flat_e = top_i.reshape(-1)
    onehot = (flat_e[:, None] == jnp.arange(n_experts, dtype=jnp.int32)[None, :]).astype(jnp.int32)
    csum = jnp.cumsum(onehot, axis=0)
    counts = csum[-1]
    rank = jnp.take_along_axis(csum, flat_e[:, None], axis=1)[:, 0] - 1
    padded = (counts + tm_e - 1) // tm_e * tm_e
    pend = jnp.cumsum(padded)
    pstart = pend - padded
    dest = (pstart[flat_e] + rank).astype(jnp.int32)
    n_slots = -(-(n_assign + n_experts * tm_e) // rows_per_step) * rows_per_step
    n_slots = -(-n_slots // tm_e) * tm_e
    flat_tok = jnp.arange(n_assign, dtype=jnp.int32) // TOP_K
    slot_tok = jnp.zeros((n_slots,), jnp.int32).at[dest].set(flat_tok, unique_indices=True)
    n_tiles = n_slots // tm_e
    tile_e = jnp.minimum(jnp.searchsorted(pend, jnp.arange(n_tiles, dtype=jnp.int32) * tm_e, side='right'),
                         n_experts - 1).astype(jnp.int32)
    n_used = (pend[-1] // tm_e).astype(jnp.int32).reshape(1)

    ys2 = _moe_ffn(tile_e, n_used, slot_tok, h2p, w1[0], b1[0], w2[0], b2[0],
                   tm=tm_e, th=min(256, d), nseg=nseg)
    tm_c = min(512, t_s)
    assert n_tok % tm_c == 0 and t_p % tm_c == 0
    dest_kmajor = dest.reshape(n_tok, TOP_K).T.reshape(-1)
    n_pad = -(-n_assign // rows_per_step) * rows_per_step
    dest_kmajor = jnp.pad(dest_kmajor, (0, n_pad - n_assign))
    yg = _gather_rows(dest_kmajor, ys2.reshape(n_slots, nseg, LANES), rows_per_step=rows_per_step)
    yg2 = yg.reshape(n_pad * nseg, LANES)

    nblk_all = n_tok // tm_c
    y_p = _combine(yg2, gates, x1_p, mods_p[5], pl.BlockSpec((None, 1, d), lambda i: (i // (seq // tm_c), 0, 0)),
                   tm=tm_c, row_off=0, nblk_all=nblk_all, nseg=nseg)
    y_s = _combine(yg2, gates, x1_s, mods_s[5], pl.BlockSpec((tm_c, d), lambda i: (i, 0)),
                   tm=tm_c, row_off=t_p // tm_c, nblk_all=nblk_all, nseg=nseg)

    return (y_p.reshape(nb_p, seq, d), y_s.reshape(nb_s, ntok, d),
            rows_p.reshape(1, nb_p, seq, cache_w), rows_s.reshape(1, nb_s, ntok, cache_w),
            st_p[None], st_s[None])
```
